```python
import math
import jax, jax.numpy as jnp
from jax import lax
import numpy as np


D_MODEL = 2048
BATCH = 1
SEQ = 8192
DEPTH = 1
DEC_BATCH = 4
DEC_SEQ = 8192
PAST_LEN = 128

GRID_W = 64
EPS = 1e-6
SSD_INNER = 2048
SSD_HEAD_DIM = 64
SSD_HEADS = SSD_INNER // SSD_HEAD_DIM
SSD_GROUPS = 4
SSD_STATE = 128
SSD_CONV = 5
SSD_CHUNK = 128
CONV_CH = SSD_INNER + 2 * SSD_GROUPS * SSD_STATE
NA_HEADS = 16
NA_HEAD_DIM = 128
NA_WIDTH = NA_HEADS * NA_HEAD_DIM
NA_KR = 8
NA_KC = 16
CA_HEADS = 4
CA_HEAD_DIM = 128
CA_WIDTH = CA_HEADS * CA_HEAD_DIM
N_MEM = 256
D_FF = -((-8 * D_MODEL) // (3 * 256)) * 256
IN_COLS = SSD_INNER + CONV_CH + 2 * SSD_HEADS + 3 * NA_WIDTH + 2 * D_MODEL

kernel_name = 'hybrid_ssd_natten_encoder'


def _rmsnorm(x, g):
    xf = x.astype(jnp.float32)
    y = xf * lax.rsqrt(jnp.mean(xf * xf, axis=-1, keepdims=True) + EPS)
    return (y * g.astype(jnp.float32)).astype(x.dtype)


def _dwconv_centred(u, w, bias):
    k = w.shape[0]
    out = lax.conv_general_dilated(u, w[:, None, :], window_strides=(1,), padding=[(k // 2, k // 2)],
                                   dimension_numbers=('NWC', 'WIO', 'NWC'), feature_group_count=u.shape[-1])
    return out + bias


def _ssd_scan(xin, dA, Bg, Cg):
    b, l, h, p = xin.shape
    g, n = Bg.shape[2], Bg.shape[3]
    e = h // g
    nc = l // SSD_CHUNK
    X = xin.reshape(b, nc, SSD_CHUNK, g, e, p)
    A = dA.reshape(b, nc, SSD_CHUNK, g, e)
    Bc = Bg.reshape(b, nc, SSD_CHUNK, g, n)
    Cc = Cg.reshape(b, nc, SSD_CHUNK, g, n)
    cs = jnp.cumsum(A, axis=2)
    tri = jnp.tril(jnp.ones((SSD_CHUNK, SSD_CHUNK), dtype=bool))
    diff = cs[:, :, :, None] - cs[:, :, None, :]
    Lmat = jnp.exp(jnp.where(tri[None, None, :, :, None, None], diff, -jnp.inf))
    G = jnp.einsum('bclgn,bcsgn->bclsg', Cc, Bc)
    y_diag = jnp.einsum('bclsge,bcsgep->bclgep', G[..., None] * Lmat, X)
    decay_states = jnp.exp(cs[:, :, -1:] - cs)
    states = jnp.einsum('bclgn,bclge,bclgep->bcgepn', Bc, decay_states, X)
    chunk_decay = jnp.exp(cs[:, :, -1])

    def step(carry, inp):
        s_c, d_c = inp
        return carry * d_c[..., None, None] + s_c, carry

    init = jnp.zeros_like(states[:, 0])
    _, prev = lax.scan(step, init, (jnp.swapaxes(states, 0, 1), jnp.swapaxes(chunk_decay, 0, 1)))
    prev = jnp.swapaxes(prev, 0, 1)
    y_off = jnp.einsum('bclgn,bcgepn,bclge->bclgep', Cc, prev, jnp.exp(cs))
    return (y_diag + y_off).reshape(b, l, h, p)


def _ssd_branch(z, xbc_raw, dt_raw, conv_w, conv_b, a_log, dt_bias, d_skip, g_ssd_out):
    b, l, _ = z.shape
    gn = SSD_GROUPS * SSD_STATE
    xbc = jax.nn.silu(_dwconv_centred(xbc_raw, conv_w, conv_b))
    xh = xbc[..., :SSD_INNER].reshape(b, l, SSD_HEADS, SSD_HEAD_DIM)
    Bm = xbc[..., SSD_INNER:SSD_INNER + gn].reshape(b, l, SSD_GROUPS, SSD_STATE)
    Cm = xbc[..., SSD_INNER + gn:].reshape(b, l, SSD_GROUPS, SSD_STATE)
    A = -jnp.exp(a_log.astype(jnp.float32))
    dt = jax.nn.softplus(dt_raw.reshape(b, l, 2, SSD_HEADS).astype(jnp.float32) + dt_bias.astype(jnp.float32))
    y_f = _ssd_scan(xh * dt[:, :, 0, :, None], dt[:, :, 0] * A[0], Bm, Cm)
    fl = lambda t: jnp.flip(t, axis=1)
    y_b = fl(_ssd_scan(fl(xh * dt[:, :, 1, :, None]), fl(dt[:, :, 1] * A[1]), fl(Bm), fl(Cm)))
    y = y_f + y_b + d_skip[:, None] * xh
    y = y.reshape(b, l, SSD_INNER).astype(z.dtype)
    return _rmsnorm(y * jax.nn.silu(z), g_ssd_out)


def _neighbourhood_attn(q, k, v, rpb):
    b, l, _ = q.shape
    rows = l // GRID_W
    kr = min(NA_KR, rows)
    shp = (b, rows, GRID_W, NA_HEADS, NA_HEAD_DIM)
    qg, kg, vg = q.reshape(shp), k.reshape(shp), v.reshape(shp)
    col = np.arange(GRID_W)
    cstart = np.clip(col - NA_KC // 2, 0, GRID_W - NA_KC)
    col_idx = cstart[:, None] + np.arange(NA_KC)[None, :]
    dc = col_idx - col[:, None]
    scale = NA_HEAD_DIM ** -0.5

    def one_row(r):
        rs = jnp.clip(r - kr // 2, 0, rows - kr)
        q_r = lax.dynamic_index_in_dim(qg, r, axis=1, keepdims=False)
        k_r = lax.dynamic_slice_in_dim(kg, rs, kr, axis=1)[:, :, col_idx]
        v_r = lax.dynamic_slice_in_dim(vg, rs, kr, axis=1)[:, :, col_idx]
        dr = rs + jnp.arange(kr) - r
        bias = jnp.take(rpb, dr + NA_KR - 1, axis=1)[:, :, dc + NA_KC - 1]
        bias = jnp.transpose(bias, (0, 2, 1, 3)).astype(jnp.float32)
        s = jnp.einsum('bchd,bicjhd->bhcij', q_r, k_r).astype(jnp.float32) * scale + bias[None]
        pr = jax.nn.softmax(s.reshape(b, NA_HEADS, GRID_W, kr * NA_KC), axis=-1)
        pr = pr.reshape(b, NA_HEADS, GRID_W, kr, NA_KC).astype(v.dtype)
        return jnp.einsum('bhcij,bicjhd->bchd', pr, v_r)

    out = lax.map(one_row, jnp.arange(rows))
    return jnp.transpose(out, (1, 0, 2, 3, 4)).reshape(b, l, NA_WIDTH)


def _mem_xattn(a, mem, g_mem, w_ca_q, w_ca_kv, w_ca_o):
    b, l, _ = a.shape
    m = _rmsnorm(mem, g_mem)
    qh = (a @ w_ca_q).reshape(b, l, CA_HEADS, CA_HEAD_DIM)
    kv = (m @ w_ca_kv).reshape(b, mem.shape[1], 2, CA_HEADS, CA_HEAD_DIM)
    kh, vh = kv[:, :, 0], kv[:, :, 1]
    s = jnp.einsum('blhd,bmhd->bhlm', qh, kh).astype(jnp.float32) * (CA_HEAD_DIM ** -0.5)
    pr = jax.nn.softmax(s, axis=-1).astype(vh.dtype)
    o = jnp.einsum('bhlm,bmhd->blhd', pr, vh).reshape(b, l, CA_WIDTH)
    return o @ w_ca_o


def _swiglu(a, w_ffn_gu, w_ffn_down):
    hgu = a @ w_ffn_gu
    return (jax.nn.silu(hgu[..., :D_FF]) * hgu[..., D_FF:]) @ w_ffn_down


def _layer(x, mem, w_in, b_gate, conv_w, conv_b, a_log, dt_bias, d_skip, g_ssd_out, rpb,
           w_br_ssd, w_br_na, w_out, g_mix_pre, g_mix_post, g_ca_pre, g_mem, w_ca_q, w_ca_kv,
           w_ca_o, g_ca_post, g_ffn_pre, w_ffn_gu, w_ffn_down, g_ffn_post):
    b, l, _ = x.shape
    a = _rmsnorm(x, g_mix_pre)
    proj = a @ w_in
    o1 = SSD_INNER
    o2 = o1 + CONV_CH
    o3 = o2 + 2 * SSD_HEADS
    o4 = o3 + NA_WIDTH
    o5 = o4 + NA_WIDTH
    o6 = o5 + NA_WIDTH
    y_ssd = _ssd_branch(proj[..., :o1], proj[..., o1:o2], proj[..., o2:o3],
                        conv_w, conv_b, a_log, dt_bias, d_skip, g_ssd_out)
    y_na = _neighbourhood_attn(proj[..., o3:o4], proj[..., o4:o5], proj[..., o5:o6], rpb)
    gates = jax.nn.sigmoid((proj[..., o6:].reshape(b, l, 2, D_MODEL) + b_gate).astype(jnp.float32)).astype(x.dtype)
    merged = gates[:, :, 0] * (y_ssd @ w_br_ssd) + gates[:, :, 1] * (y_na @ w_br_na)
    x = x + _rmsnorm(merged @ w_out, g_mix_post)
    c = _mem_xattn(_rmsnorm(x, g_ca_pre), mem, g_mem, w_ca_q, w_ca_kv, w_ca_o)
    x = x + _rmsnorm(c, g_ca_post)
    f = _swiglu(_rmsnorm(x, g_ffn_pre), w_ffn_gu, w_ffn_down)
    return x + _rmsnorm(f, g_ffn_post)


def setup_inputs(seed: int = 0) -> dict:
    key = jax.random.key(seed)
    ks = jax.random.split(key, 28)
    f32 = jnp.float32
    nrm = lambda k, shape, fan_in: jax.random.normal(k, shape, f32) * (fan_in ** -0.5)
    gain = lambda k, shape: 1.0 + 0.02 * jax.random.normal(k, shape, f32)
    dt0 = jnp.exp(jax.random.uniform(ks[9], (DEPTH, 2, SSD_HEADS), f32, minval=math.log(1e-3), maxval=math.log(1e-1)))
    return {
        'x_prompt': jax.random.normal(ks[0], (BATCH, SEQ, D_MODEL), f32),
        'x_sample': jax.random.normal(ks[1], (DEC_BATCH, DEC_SEQ, D_MODEL), f32),
        'mem_prompt': jax.random.normal(ks[2], (BATCH, N_MEM, D_MODEL), f32),
        'mem_sample': jax.random.normal(ks[3], (DEC_BATCH, N_MEM, D_MODEL), f32),
        'w_in': nrm(ks[4], (DEPTH, D_MODEL, IN_COLS), D_MODEL),
        'b_gate': 0.01 * jax.random.normal(ks[5], (DEPTH, 2, D_MODEL), f32),
        'conv_w': nrm(ks[6], (DEPTH, SSD_CONV, CONV_CH), SSD_CONV),
        'conv_b': 0.01 * jax.random.normal(ks[7], (DEPTH, CONV_CH), f32),
        'a_log': jnp.log(jax.random.uniform(ks[8], (DEPTH, 2, SSD_HEADS), f32, minval=1.0, maxval=16.0)),
        'dt_bias': dt0 + jnp.log(-jnp.expm1(-dt0)),
        'd_skip': gain(ks[10], (DEPTH, SSD_HEADS)),
        'g_ssd_out': gain(ks[11], (DEPTH, SSD_INNER)),
        'rpb': 0.02 * jax.random.normal(ks[12], (DEPTH, NA_HEADS, 2 * NA_KR - 1, 2 * NA_KC - 1), f32),
        'w_br_ssd': nrm(ks[13], (DEPTH, SSD_INNER, D_MODEL), SSD_INNER),
        'w_br_na': nrm(ks[14], (DEPTH, NA_WIDTH, D_MODEL), NA_WIDTH),
        'w_out': nrm(ks[15], (DEPTH, D_MODEL, D_MODEL), D_MODEL),
        'g_mix_pre': gain(ks[16], (DEPTH, D_MODEL)),
        'g_mix_post': gain(ks[17], (DEPTH, D_MODEL)),
        'g_ca_pre': gain(ks[18], (DEPTH, D_MODEL)),
        'g_mem': gain(ks[19], (DEPTH, D_MODEL)),
        'w_ca_q': nrm(ks[20], (DEPTH, D_MODEL, CA_WIDTH), D_MODEL),
        'w_ca_kv': nrm(ks[21], (DEPTH, D_MODEL, 2 * CA_WIDTH), D_MODEL),
        'w_ca_o': nrm(ks[22], (DEPTH, CA_WIDTH, D_MODEL), CA_WIDTH),
        'g_ca_post': gain(ks[23], (DEPTH, D_MODEL)),
        'g_ffn_pre': gain(ks[24], (DEPTH, D_MODEL)),
        'w_ffn_gu': nrm(ks[25], (DEPTH, D_MODEL, 2 * D_FF), D_MODEL),
        'w_ffn_down': nrm(ks[26], (DEPTH, D_FF, D_MODEL), D_FF),
        'g_ffn_post': gain(ks[27], (DEPTH, D_MODEL)),
    }


def reference(x_prompt, x_sample, mem_prompt, mem_sample, w_in, b_gate, conv_w, conv_b, a_log,
              dt_bias, d_skip, g_ssd_out, rpb, w_br_ssd, w_br_na, w_out, g_mix_pre, g_mix_post,
              g_ca_pre, g_mem, w_ca_q, w_ca_kv, w_ca_o, g_ca_post, g_ffn_pre, w_ffn_gu, w_ffn_down,
              g_ffn_post):
    params = (w_in, b_gate, conv_w, conv_b, a_log, dt_bias, d_skip, g_ssd_out, rpb, w_br_ssd,
              w_br_na, w_out, g_mix_pre, g_mix_post, g_ca_pre, g_mem, w_ca_q, w_ca_kv, w_ca_o,
              g_ca_post, g_ffn_pre, w_ffn_gu, w_ffn_down, g_ffn_post)
    y_prompt = x_prompt
    y_sample = x_sample
    for i in range(DEPTH):
        p = [w[i] for w in params]
        y_prompt = _layer(y_prompt, mem_prompt, *p)
        y_sample = _layer(y_sample, mem_sample, *p)
    return (y_prompt, y_sample)
```

```python
import functools

import numpy as np
import jax
import jax.numpy as jnp
from jax import lax
from jax.experimental import pallas as pl
from jax.experimental.pallas import tpu as pltpu

BF16 = jnp.bfloat16
F32 = jnp.float32

EPS = 1e-6
GRID_W = 64
SSD_HEADS = 32
SSD_HEAD_DIM = 64
SSD_GROUPS = 4
SSD_STATE = 128
SSD_INNER = SSD_HEADS * SSD_HEAD_DIM
SSD_CONV = 5
CONV_CH = SSD_INNER + 2 * SSD_GROUPS * SSD_STATE
CHUNK = 128
HALO = 16
NA_HEADS = 16
NA_HEAD_DIM = 128
NA_KR = 8
NA_KC = 16
NA_ROWS_PER_STEP = 8
NA_KEY_ROWS = 16
CA_HEADS = 4
CA_HEAD_DIM = 128
NEG = -1e30

VMEM_LIMIT_BYTES = 56 * 1024 * 1024


def _cparams(*sem):
    return pltpu.CompilerParams(dimension_semantics=sem, vmem_limit_bytes=VMEM_LIMIT_BYTES)


def _rms(x, g):
    return x * lax.rsqrt(jnp.mean(x * x, axis=-1, keepdims=True) + EPS) * g


def _dot(a, b):
    return jnp.dot(a, b, preferred_element_type=F32)


def _dot_nt(a, b):
    return lax.dot_general(a, b, (((1,), (1,)), ((), ())), preferred_element_type=F32)


def _norm_matmul_kernel(x_ref, g_ref, w_ref, o_ref, a_scr):
    @pl.when(pl.program_id(1) == 0)
    def _():
        a_scr[...] = _rms(x_ref[...], g_ref[...]).astype(BF16)

    o_ref[...] = _dot(a_scr[...], w_ref[...]).astype(o_ref.dtype)


def _norm_matmul(x, g, w, out_dtype, tm, tn):
    m, d = x.shape
    n = w.shape[1]
    return pl.pallas_call(
        _norm_matmul_kernel,
        grid=(m // tm, n // tn),
        in_specs=[
            pl.BlockSpec((tm, d), lambda i, j: (i, 0)),
            pl.BlockSpec((1, d), lambda i, j: (0, 0)),
            pl.BlockSpec((d, tn), lambda i, j: (0, j)),
        ],
        out_specs=pl.BlockSpec((tm, tn), lambda i, j: (i, j)),
        out_shape=jax.ShapeDtypeStruct((m, n), out_dtype),
        scratch_shapes=[pltpu.VMEM((tm, d), BF16)],
        compiler_params=_cparams("parallel", "arbitrary"),
        name="norm_matmul",
    )(x, g, w)


def _split3(x):
    hi = x.astype(BF16)
    r1 = x - hi.astype(F32)
    mid = r1.astype(BF16)
    lo = (r1 - mid.astype(F32)).astype(BF16)
    return hi, mid, lo


def _ssd_kernel(xm_ref, xp_ref, xn_ref, dt_ref, cw_ref, cb_ref, alog_ref, dtb_ref, dsk_ref,
                y_ref, xs_scr, xc_scr, st_scr, *, nc):
    d = pl.program_id(1)
    c = pl.program_id(2)
    cc = jnp.where(d == 0, c, nc - 1 - c)

    @pl.when(c == 0)
    def _():
        st_scr[...] = jnp.zeros_like(st_scr)

    has_prev = (cc > 0).astype(F32)
    has_next = (cc < nc - 1).astype(F32)
    xs_scr[0:HALO, :] = xp_ref[0].astype(F32) * has_prev
    xs_scr[HALO:HALO + CHUNK, :] = xm_ref[0].astype(F32)
    xs_scr[HALO + CHUNK:, :] = xn_ref[0].astype(F32) * has_next
    acc = jnp.broadcast_to(cb_ref[...], (CHUNK, CONV_CH))
    for k in range(SSD_CONV):
        off = HALO - SSD_CONV // 2 + k
        acc = acc + cw_ref[k:k + 1, :] * xs_scr[off:off + CHUNK, :]
    xc_scr[...] = acc * jax.nn.sigmoid(acc)

    a_row = -jnp.exp(alog_ref[0])
    z = dt_ref[0] + dtb_ref[0]
    dt = jnp.maximum(z, 0.0) + jnp.log1p(jnp.exp(-jnp.abs(z)))
    da = dt * a_row

    ri = lax.broadcasted_iota(jnp.int32, (CHUNK, CHUNK), 0)
    ci = lax.broadcasted_iota(jnp.int32, (CHUNK, CHUNK), 1)
    causal = jnp.where(d == 0, ri - ci, ci - ri) >= 0
    tmat = causal.astype(BF16)
    hi, mid, lo = _split3(da)
    cs = _dot(tmat, hi) + _dot(tmat, mid) + _dot(tmat, lo)
    cs_t = cs.T
    dt_t = dt.T
    tot_col = jnp.where(d == 0, cs_t[:, CHUNK - 1:CHUNK], cs_t[:, 0:1])
    tot_b = jnp.broadcast_to(tot_col, (CHUNK, CHUNK))
    wst_t = jnp.exp(tot_b - cs_t) * dt_t
    cdec = jnp.exp(tot_b)

    gn = SSD_GROUPS * SSD_STATE
    for g in range(SSD_GROUPS):
        b_g = xc_scr[:, SSD_INNER + g * SSD_STATE:SSD_INNER + (g + 1) * SSD_STATE]
        c_g = xc_scr[:, SSD_INNER + gn + g * SSD_STATE:SSD_INNER + gn + (g + 1) * SSD_STATE]
        b_bf = b_g.astype(BF16)
        gmat = _dot_nt(c_g.astype(BF16), b_bf)
        b_t = b_g.T
        for e in range(SSD_HEADS // SSD_GROUPS):
            h = g * (SSD_HEADS // SSD_GROUPS) + e
            hs = slice(h * SSD_HEAD_DIM, (h + 1) * SSD_HEAD_DIM)
            xh = xc_scr[:, hs]
            xh_bf = xh.astype(BF16)
            s_old = st_scr[:, hs]
            col = jnp.broadcast_to(cs[:, h:h + 1], (CHUNK, CHUNK))
            row = cs_t[h:h + 1, :]
            lmat = jnp.exp(jnp.where(causal, col - row, NEG))
            w_diag = gmat * lmat * dt_t[h:h + 1, :]
            w_off = c_g * jnp.exp(col)
            lhs = jnp.concatenate([w_diag.astype(BF16), w_off.astype(BF16)], axis=1)
            rhs = jnp.concatenate([xh_bf, s_old.astype(BF16)], axis=0)
            y = _dot(lhs, rhs)
            y = y + jnp.where(d == 0, dsk_ref[:, hs], 0.0) * xh
            y_ref[0, 0, :, hs] = y.astype(y_ref.dtype)
            s_new = _dot((b_t * wst_t[h:h + 1, :]).astype(BF16), xh_bf)
            st_scr[:, hs] = s_old * cdec[h:h + 1, 0:SSD_HEAD_DIM] + s_new


def _ssd(proj, dt_raw, conv_w, conv_b, a_log, dt_bias, d_skip, xbc_block):
    b, l, _ = proj.shape
    nc = l // CHUNK
    hpc = CHUNK // HALO

    def chunk(d, c):
        return jnp.where(d == 0, c, nc - 1 - c)

    kern = functools.partial(_ssd_kernel, nc=nc)
    return pl.pallas_call(
        kern,
        grid=(b, 2, nc),
        in_specs=[
            pl.BlockSpec((1, CHUNK, CONV_CH), lambda i, d, c: (i, chunk(d, c), xbc_block)),
            pl.BlockSpec((1, HALO, CONV_CH),
                         lambda i, d, c: (i, jnp.maximum(chunk(d, c) * hpc - 1, 0), xbc_block)),
            pl.BlockSpec((1, HALO, CONV_CH),
                         lambda i, d, c: (i, jnp.minimum((chunk(d, c) + 1) * hpc, nc * hpc - 1), xbc_block)),
            pl.BlockSpec((1, CHUNK, 128), lambda i, d, c: (i, chunk(d, c), d)),
            pl.BlockSpec((8, CONV_CH), lambda i, d, c: (0, 0)),
            pl.BlockSpec((1, CONV_CH), lambda i, d, c: (0, 0)),
            pl.BlockSpec((1, 1, 128), lambda i, d, c: (d, 0, 0)),
            pl.BlockSpec((1, 1, 128), lambda i, d, c: (d, 0, 0)),
            pl.BlockSpec((1, SSD_INNER), lambda i, d, c: (0, 0)),
        ],
        out_specs=pl.BlockSpec((1, 1, CHUNK, SSD_INNER), lambda i, d, c: (d, i, chunk(d, c), 0)),
        out_shape=jax.ShapeDtypeStruct((2, b, l, SSD_INNER), BF16),
        scratch_shapes=[
            pltpu.VMEM((CHUNK + 2 * HALO, CONV_CH), F32),
            pltpu.VMEM((CHUNK, CONV_CH), F32),
            pltpu.VMEM((SSD_STATE, SSD_INNER), F32),
        ],
        compiler_params=_cparams("parallel", "arbitrary", "arbitrary"),
        name="ssd_scan",
    )(proj, proj, proj, dt_raw, conv_w, conv_b, a_log, dt_bias, d_skip)


def _na_bias_layout(case, qr):
    half = NA_KR // 2
    if case == 0:
        rs = max(qr - half, 0)
        return rs - qr + NA_KR - 1, rs
    if case == 1:
        return NA_KR - 1 - half, qr
    rs = min(qr + half, NA_KEY_ROWS - NA_KR)
    return rs - (NA_KEY_ROWS - NA_ROWS_PER_STEP + qr) + NA_KR - 1, rs


def _na_kernel(q_ref, k_ref, v_ref, strip_ref, o_ref, bias_scr, *, n_steps, rows):
    b = pl.program_id(1)
    i = pl.program_id(2)
    nq = NA_ROWS_PER_STEP * GRID_W
    nk = NA_KEY_ROWS * GRID_W
    win = NA_KR * GRID_W

    @pl.when((b == 0) & (i == 0))
    def _():
        bias_scr[...] = jnp.full(bias_scr.shape, NEG, F32)
        for case in range(3):
            for qr in range(NA_ROWS_PER_STEP):
                o, rs = _na_bias_layout(case, qr)
                bias_scr[case, qr * GRID_W:(qr + 1) * GRID_W, rs * GRID_W:rs * GRID_W + win] = strip_ref[0, o]

    case = jnp.where(i == 0, 0, jnp.where(i == n_steps - 1, 2, 1))
    ks = jnp.clip(i * NA_ROWS_PER_STEP - NA_KR // 2, 0, rows - NA_KEY_ROWS)
    k0 = pl.multiple_of(ks * GRID_W, GRID_W)
    kw = k_ref[0, pl.ds(k0, nk), :]
    vw = v_ref[0, pl.ds(k0, nk), :]
    s = _dot_nt(q_ref[0], kw) * (NA_HEAD_DIM ** -0.5) + bias_scr[case]
    m = jnp.max(s, axis=-1, keepdims=True)
    p = jnp.exp(s - m)
    den = jnp.sum(p, axis=-1, keepdims=True)
    o_ref[0] = (_dot(p.astype(BF16), vw) / den).astype(o_ref.dtype)


def _na(proj, strips, q_block, k_block, v_block):
    b, l, _ = proj.shape
    rows = l // GRID_W
    n_steps = rows // NA_ROWS_PER_STEP
    nq = NA_ROWS_PER_STEP * GRID_W
    win = NA_KR * GRID_W
    kern = functools.partial(_na_kernel, n_steps=n_steps, rows=rows)
    return pl.pallas_call(
        kern,
        grid=(NA_HEADS, b, n_steps),
        in_specs=[
            pl.BlockSpec((1, nq, NA_HEAD_DIM), lambda h, bi, i: (bi, i, q_block + h)),
            pl.BlockSpec((1, l, NA_HEAD_DIM), lambda h, bi, i: (bi, 0, k_block + h)),
            pl.BlockSpec((1, l, NA_HEAD_DIM), lambda h, bi, i: (bi, 0, v_block + h)),
            pl.BlockSpec((1, NA_KR, GRID_W, win), lambda h, bi, i: (h, 0, 0, 0)),
        ],
        out_specs=pl.BlockSpec((1, nq, NA_HEAD_DIM), lambda h, bi, i: (bi, i, h)),
        out_shape=jax.ShapeDtypeStruct((b, l, NA_HEADS * NA_HEAD_DIM), BF16),
        scratch_shapes=[pltpu.VMEM((3, nq, NA_KEY_ROWS * GRID_W), F32)],
        compiler_params=_cparams("arbitrary", "arbitrary", "arbitrary"),
        name="neighbourhood_attn",
    )(proj, proj, proj, strips)


def _na_strips(rpb):
    qc = np.arange(GRID_W)[:, None]
    kc = np.arange(GRID_W)[None, :]
    cstart = np.clip(qc - NA_KC // 2, 0, GRID_W - NA_KC)
    valid = (kc >= cstart) & (kc < cstart + NA_KC)
    dcidx = np.clip(kc - qc + NA_KC - 1, 0, 2 * NA_KC - 2)
    tb = jnp.where(valid[None, None], rpb[:, :, dcidx].astype(F32), NEG)
    return jnp.stack(
        [jnp.concatenate([tb[:, o + j] for j in range(NA_KR)], axis=-1) for o in range(NA_KR)], axis=1)


def _merge_kernel(yf_ref, yb_ref, z_ref, yna_ref, gs_ref, gn_ref, gso_ref, bg_ref, w1_ref, w2_ref,
                  o_ref, u_scr):
    @pl.when(pl.program_id(1) == 0)
    def _():
        y = yf_ref[0].astype(F32) + yb_ref[0].astype(F32)
        zf = z_ref[...].astype(F32)
        u_scr[...] = _rms(y * (zf * jax.nn.sigmoid(zf)), gso_ref[...]).astype(BF16)

    g_ssd = jax.nn.sigmoid(gs_ref[...].astype(F32) + bg_ref[0:1, :])
    g_na = jax.nn.sigmoid(gn_ref[...].astype(F32) + bg_ref[1:2, :])
    o_ref[...] = (g_ssd * _dot(u_scr[...], w1_ref[...])
                  + g_na * _dot(yna_ref[...], w2_ref[...])).astype(o_ref.dtype)


def _merge(y2, proj, yna, g_ssd_out, b_gate, w1, w2, z_block, gs_block, gn_block, tm, tn):
    _, m, d = y2.shape
    r = d // tn
    return pl.pallas_call(
        _merge_kernel,
        grid=(m // tm, d // tn),
        in_specs=[
            pl.BlockSpec((1, tm, d), lambda i, j: (0, i, 0)),
            pl.BlockSpec((1, tm, d), lambda i, j: (1, i, 0)),
            pl.BlockSpec((tm, d), lambda i, j: (i, z_block)),
            pl.BlockSpec((tm, d), lambda i, j: (i, 0)),
            pl.BlockSpec((tm, tn), lambda i, j: (i, gs_block * r + j)),
            pl.BlockSpec((tm, tn), lambda i, j: (i, gn_block * r + j)),
            pl.BlockSpec((1, d), lambda i, j: (0, 0)),
            pl.BlockSpec((2, tn), lambda i, j: (0, j)),
            pl.BlockSpec((d, tn), lambda i, j: (0, j)),
            pl.BlockSpec((d, tn), lambda i, j: (0, j)),
        ],
        out_specs=pl.BlockSpec((tm, tn), lambda i, j: (i, j)),
        out_shape=jax.ShapeDtypeStruct((m, d), BF16),
        scratch_shapes=[pltpu.VMEM((tm, d), BF16)],
        compiler_params=_cparams("parallel", "arbitrary"),
        name="branch_merge",
    )(y2, y2, proj, yna, proj, proj, g_ssd_out, b_gate, w1, w2)


def _proj_norm_res_kernel(a_ref, w_ref, g_ref, x_ref, o_ref):
    o_ref[...] = x_ref[...] + _rms(_dot(a_ref[...], w_ref[...]), g_ref[...])


def _proj_norm_res(a, w, g, x, tm):
    m, k = a.shape
    d = w.shape[1]
    return pl.pallas_call(
        _proj_norm_res_kernel,
        grid=(m // tm,),
        in_specs=[
            pl.BlockSpec((tm, k), lambda i: (i, 0)),
            pl.BlockSpec((k, d), lambda i: (0, 0)),
            pl.BlockSpec((1, d), lambda i: (0, 0)),
            pl.BlockSpec((tm, d), lambda i: (i, 0)),
        ],
        out_specs=pl.BlockSpec((tm, d), lambda i: (i, 0)),
        out_shape=jax.ShapeDtypeStruct((m, d), F32),
        compiler_params=_cparams("parallel"),
        name="proj_norm_residual",
    )(a, w, g, x)


def _xattn_kernel(x_ref, kv_ref, gpre_ref, wq_ref, wo_ref, gpost_ref, o_ref):
    x = x_ref[0]
    a = _rms(x, gpre_ref[...]).astype(BF16)
    q = _dot(a, wq_ref[...]).astype(BF16)
    width = CA_HEADS * CA_HEAD_DIM
    outs = []
    for h in range(CA_HEADS):
        hs = slice(h * CA_HEAD_DIM, (h + 1) * CA_HEAD_DIM)
        kh = kv_ref[0, :, hs]
        vh = kv_ref[0, :, width + h * CA_HEAD_DIM:width + (h + 1) * CA_HEAD_DIM]
        s = _dot_nt(q[:, hs], kh) * (CA_HEAD_DIM ** -0.5)
        m = jnp.max(s, axis=-1, keepdims=True)
        p = jnp.exp(s - m)
        den = jnp.sum(p, axis=-1, keepdims=True)
        outs.append((_dot(p.astype(BF16), vh) / den).astype(BF16))
    o = jnp.concatenate(outs, axis=1)
    o_ref[0] = x + _rms(_dot(o, wo_ref[...]), gpost_ref[...])


def _xattn(x, kv, g_pre, wq, wo, g_post, tm):
    b, l, d = x.shape
    n_mem, kvw = kv.shape[1], kv.shape[2]
    width = wq.shape[1]
    return pl.pallas_call(
        _xattn_kernel,
        grid=(b, l // tm),
        in_specs=[
            pl.BlockSpec((1, tm, d), lambda bi, i: (bi, i, 0)),
            pl.BlockSpec((1, n_mem, kvw), lambda bi, i: (bi, 0, 0)),
            pl.BlockSpec((1, d), lambda bi, i: (0, 0)),
            pl.BlockSpec((d, width), lambda bi, i: (0, 0)),
            pl.BlockSpec((width, d), lambda bi, i: (0, 0)),
            pl.BlockSpec((1, d), lambda bi, i: (0, 0)),
        ],
        out_specs=pl.BlockSpec((1, tm, d), lambda bi, i: (bi, i, 0)),
        out_shape=jax.ShapeDtypeStruct((b, l, d), F32),
        compiler_params=_cparams("parallel", "parallel"),
        name="mem_xattn",
    )(x, kv, g_pre, wq, wo, g_post)


def _ffn_kernel(x_ref, gpre_ref, wg_ref, wu_ref, wd_ref, gpost_ref, o_ref, a_scr, acc_scr):
    f = pl.program_id(1)

    @pl.when(f == 0)
    def _():
        a_scr[...] = _rms(x_ref[...], gpre_ref[...]).astype(BF16)
        acc_scr[...] = jnp.zeros_like(acc_scr)

    a = a_scr[...]
    hg = _dot(a, wg_ref[...])
    hu = _dot(a, wu_ref[...])
    act = (hg * jax.nn.sigmoid(hg) * hu).astype(BF16)
    acc_scr[...] += _dot(act, wd_ref[...])

    @pl.when(f == pl.num_programs(1) - 1)
    def _():
        o_ref[...] = x_ref[...] + _rms(acc_scr[...], gpost_ref[...])


def _ffn(x, g_pre, w_gu, w_down, g_post, tm, tf):
    m, d = x.shape
    dff = w_down.shape[0]
    nf = dff // tf
    return pl.pallas_call(
        _ffn_kernel,
        grid=(m // tm, nf),
        in_specs=[
            pl.BlockSpec((tm, d), lambda i, f: (i, 0)),
            pl.BlockSpec((1, d), lambda i, f: (0, 0)),
            pl.BlockSpec((d, tf), lambda i, f: (0, f)),
            pl.BlockSpec((d, tf), lambda i, f: (0, nf + f)),
            pl.BlockSpec((tf, d), lambda i, f: (f, 0)),
            pl.BlockSpec((1, d), lambda i, f: (0, 0)),
        ],
        out_specs=pl.BlockSpec((tm, d), lambda i, f: (i, 0)),
        out_shape=jax.ShapeDtypeStruct((m, d), F32),
        scratch_shapes=[pltpu.VMEM((tm, d), BF16), pltpu.VMEM((tm, d), F32)],
        compiler_params=_cparams("parallel", "arbitrary"),
        name="swiglu_ffn",
    )(x, g_pre, w_gu, w_gu, w_down, g_post)


def _pick(n, target):
    t = min(n, target)
    while n % t:
        t //= 2
    return t


def _layer(x, mem, w_in, b_gate, conv_w, conv_b, a_log, dt_bias, d_skip, g_ssd_out, rpb,
           w_br_ssd, w_br_na, w_out, g_mix_pre, g_mix_post, g_ca_pre, g_mem, w_ca_q, w_ca_kv,
           w_ca_o, g_ca_post, g_ffn_pre, w_ffn_gu, w_ffn_down, g_ffn_post):
    b, l, d = x.shape
    m = b * l
    row = lambda v: v.reshape(1, -1).astype(F32)

    o1 = SSD_INNER
    o2 = o1 + CONV_CH
    o3 = o2 + 2 * SSD_HEADS
    na_w = NA_HEADS * NA_HEAD_DIM
    w_main = jnp.concatenate([w_in[:, :o1], w_in[:, o3:], w_in[:, o1:o2]], axis=1).astype(BF16)
    lane_pad = jnp.zeros((d, 128 - SSD_HEADS), w_in.dtype)
    w_dt = jnp.concatenate([w_in[:, o2:o2 + SSD_HEADS], lane_pad, w_in[:, o2 + SSD_HEADS:o3], lane_pad],
                           axis=1).astype(BF16)
    z_block, q_col, k_col, v_col = 0, d, d + na_w, d + 2 * na_w
    gs_col = d + 3 * na_w
    gn_col = gs_col + d
    xbc_col = gn_col + d
    assert xbc_col % CONV_CH == 0 and gs_col % d == 0 and gn_col % d == 0

    x2d = x.reshape(m, d)
    g_pre = row(g_mix_pre)
    proj = _norm_matmul(x2d, g_pre, w_main, BF16, _pick(m, 1024), 1024)
    dt_raw = _norm_matmul(x2d, g_pre, w_dt, F32, _pick(m, 1024), 256)

    head_pad = jnp.zeros((2, 1, 128 - SSD_HEADS), F32)
    y2 = _ssd(proj.reshape(b, l, -1), dt_raw.reshape(b, l, 256),
              jnp.pad(conv_w.astype(F32), ((0, 8 - SSD_CONV), (0, 0))), row(conv_b),
              jnp.concatenate([a_log.astype(F32).reshape(2, 1, SSD_HEADS), head_pad], axis=2),
              jnp.concatenate([dt_bias.astype(F32).reshape(2, 1, SSD_HEADS), head_pad], axis=2),
              row(jnp.repeat(d_skip, SSD_HEAD_DIM)), xbc_col // CONV_CH)

    y_na = _na(proj.reshape(b, l, -1), _na_strips(rpb), q_col // NA_HEAD_DIM, k_col // NA_HEAD_DIM,
               v_col // NA_HEAD_DIM)

    merged = _merge(y2.reshape(2, m, SSD_INNER), proj, y_na.reshape(m, na_w), row(g_ssd_out),
                    b_gate.astype(F32), w_br_ssd.astype(BF16), w_br_na.astype(BF16),
                    z_block, gs_col // d, gn_col // d, _pick(m, 512), 1024)
    x1 = _proj_norm_res(merged, w_out.astype(BF16), row(g_mix_post), x2d, _pick(m, 512))

    n_mem = mem.shape[1]
    kv = _norm_matmul(mem.reshape(b * n_mem, d), row(g_mem), w_ca_kv.astype(BF16), BF16,
                      _pick(b * n_mem, 256), 1024)
    x2 = _xattn(x1.reshape(b, l, d), kv.reshape(b, n_mem, -1), row(g_ca_pre), w_ca_q.astype(BF16),
                w_ca_o.astype(BF16), row(g_ca_post), _pick(l, 512))

    x3 = _ffn(x2.reshape(m, d), row(g_ffn_pre), w_ffn_gu.astype(BF16), w_ffn_down.astype(BF16),
              row(g_ffn_post), _pick(m, 512), 512)
    return x3.reshape(b, l, d)


def kernel(x_prompt, x_sample, mem_prompt, mem_sample, w_in, b_gate, conv_w, conv_b, a_log, dt_bias, d_skip, g_ssd_out, rpb, w_br_ssd, w_br_na, w_out, g_mix_pre, g_mix_post, g_ca_pre, g_mem, w_ca_q, w_ca_kv, w_ca_o, g_ca_post, g_ffn_pre, w_ffn_gu, w_ffn_down, g_ffn_post):
    params = (w_in, b_gate, conv_w, conv_b, a_log, dt_bias, d_skip, g_ssd_out, rpb, w_br_ssd,
              w_br_na, w_out, g_mix_pre, g_mix_post, g_ca_pre, g_mem, w_ca_q, w_ca_kv, w_ca_o,
              g_ca_post, g_ffn_pre, w_ffn_gu, w_ffn_down, g_ffn_post)
    nb = x_prompt.shape[0]
    y = jnp.concatenate([x_prompt, x_sample], axis=0)
    mem = jnp.concatenate([mem_prompt, mem_sample], axis=0)
    for i in range(w_in.shape[0]):
        y = _layer(y, mem, *[w[i] for w in params])
    return (y[:nb], y[nb:])
```

```python
import functools
import math

import numpy as np
import jax
import jax.numpy as jnp
from jax import lax
from jax.experimental import pallas as pl
from jax.experimental.pallas import tpu as pltpu

BF16 = jnp.bfloat16
F32 = jnp.float32

EPS = 1e-6
LANES = 128
GRID_W = 64
SSD_HEADS = 32
SSD_HEAD_DIM = 64
SSD_GROUPS = 4
SSD_STATE = 128
SSD_INNER = SSD_HEADS * SSD_HEAD_DIM
SSD_CONV = 5
CONV_CH = SSD_INNER + 2 * SSD_GROUPS * SSD_STATE
CHUNK = 128
HALO = 16
NA_HEADS = 16
NA_HEAD_DIM = 128
NA_KR = 8
NA_KC = 16
CA_HEADS = 4
CA_HEAD_DIM = 128
NEG = -1e30
LOG2E = math.log2(math.e)

VMEM_LIMIT_BYTES = 56 * 1024 * 1024


def _cparams(*sem):
    return pltpu.CompilerParams(dimension_semantics=sem, vmem_limit_bytes=VMEM_LIMIT_BYTES)


def _rms(x, g):
    return x * lax.rsqrt(jnp.mean(x * x, axis=-1, keepdims=True) + EPS) * g


def _dot(a, b):
    return jnp.dot(a, b, preferred_element_type=F32)


def _dot_nt(a, b):
    return lax.dot_general(a, b, (((1,), (1,)), ((), ())), preferred_element_type=F32)


def _norm_matmul_kernel(x_ref, g_ref, w_ref, o_ref, a_scr):
    @pl.when(pl.program_id(1) == 0)
    def _():
        a_scr[...] = _rms(x_ref[...], g_ref[...]).astype(BF16)

    o_ref[...] = _dot(a_scr[...], w_ref[...]).astype(o_ref.dtype)


def _norm_matmul(x, g, w, out_dtype, tm, tn):
    m, d = x.shape
    n = w.shape[1]
    return pl.pallas_call(
        _norm_matmul_kernel,
        grid=(m // tm, n // tn),
        in_specs=[
            pl.BlockSpec((tm, d), lambda i, j: (i, 0)),
            pl.BlockSpec((1, d), lambda i, j: (0, 0)),
            pl.BlockSpec((d, tn), lambda i, j: (0, j)),
        ],
        out_specs=pl.BlockSpec((tm, tn), lambda i, j: (i, j)),
        out_shape=jax.ShapeDtypeStruct((m, n), out_dtype),
        scratch_shapes=[pltpu.VMEM((tm, d), BF16)],
        compiler_params=_cparams("parallel", "arbitrary"),
        name="norm_matmul",
    )(x, g, w)


def _split3(x):
    hi = x.astype(BF16)
    r1 = x - hi.astype(F32)
    mid = r1.astype(BF16)
    lo = (r1 - mid.astype(F32)).astype(BF16)
    return hi, mid, lo


def _dt_kernel(x_ref, g_ref, w_ref, alog_ref, dtb_ref, dt_ref, cs_ref):
    tm = x_ref.shape[0]
    a = _rms(x_ref[...], g_ref[...]).astype(BF16)
    z = _dot(a, w_ref[...]) + dtb_ref[...]
    dt = jnp.maximum(z, 0.0) + jnp.log1p(jnp.exp(-jnp.abs(z)))
    da = dt * -jnp.exp(alog_ref[...])
    ri = lax.broadcasted_iota(jnp.int32, (tm, tm), 0)
    ci = lax.broadcasted_iota(jnp.int32, (tm, tm), 1)
    same = (ri // CHUNK) == (ci // CHUNK)
    t_fwd = (same & (ri >= ci)).astype(BF16)
    t_bwd = (same & (ri <= ci)).astype(BF16)
    hi, mid, lo = _split3(da)
    cs_f = _dot(t_fwd, hi[:, :LANES]) + _dot(t_fwd, mid[:, :LANES]) + _dot(t_fwd, lo[:, :LANES])
    cs_b = _dot(t_bwd, hi[:, LANES:]) + _dot(t_bwd, mid[:, LANES:]) + _dot(t_bwd, lo[:, LANES:])
    dt_ref[...] = dt
    cs_ref[:, :LANES] = cs_f * LOG2E
    cs_ref[:, LANES:] = cs_b * LOG2E


def _dt_scan(x, g, w_dt, a_log, dt_bias, tm):
    m, d = x.shape
    n = 2 * LANES
    shp = jax.ShapeDtypeStruct((m, n), F32)
    return pl.pallas_call(
        _dt_kernel,
        grid=(m // tm,),
        in_specs=[
            pl.BlockSpec((tm, d), lambda i: (i, 0)),
            pl.BlockSpec((1, d), lambda i: (0, 0)),
            pl.BlockSpec((d, n), lambda i: (0, 0)),
            pl.BlockSpec((1, n), lambda i: (0, 0)),
            pl.BlockSpec((1, n), lambda i: (0, 0)),
        ],
        out_specs=[pl.BlockSpec((tm, n), lambda i: (i, 0)), pl.BlockSpec((tm, n), lambda i: (i, 0))],
        out_shape=[shp, shp],
        compiler_params=_cparams("parallel"),
        name="ssd_dt_scan",
    )(x, g, w_dt, a_log, dt_bias)


def _conv_kernel(xm_ref, xp_ref, xn_ref, cw_ref, cb_ref, o_ref, xs_scr, *, n_tiles):
    i = pl.program_id(1)
    tc = xm_ref.shape[1]
    has_prev = (i > 0).astype(F32)
    has_next = (i < n_tiles - 1).astype(F32)
    xs_scr[0:HALO, :] = xp_ref[0].astype(F32) * has_prev
    xs_scr[HALO:HALO + tc, :] = xm_ref[0].astype(F32)
    xs_scr[HALO + tc:, :] = xn_ref[0].astype(F32) * has_next
    for r in range(tc // CHUNK):
        acc = jnp.broadcast_to(cb_ref[...], (CHUNK, CONV_CH))
        for k in range(SSD_CONV):
            off = r * CHUNK + HALO - SSD_CONV // 2 + k
            acc = acc + cw_ref[k:k + 1, :] * xs_scr[off:off + CHUNK, :]
        o_ref[0, r * CHUNK:(r + 1) * CHUNK, :] = (acc * jax.nn.sigmoid(acc)).astype(o_ref.dtype)


def _conv_silu(proj, conv_w, conv_b, xbc_block, tc):
    b, l, _ = proj.shape
    n_tiles = l // tc
    hpt = tc // HALO
    kern = functools.partial(_conv_kernel, n_tiles=n_tiles)
    return pl.pallas_call(
        kern,
        grid=(b, n_tiles),
        in_specs=[
            pl.BlockSpec((1, tc, CONV_CH), lambda bi, i: (bi, i, xbc_block)),
            pl.BlockSpec((1, HALO, CONV_CH), lambda bi, i: (bi, jnp.maximum(i * hpt - 1, 0), xbc_block)),
            pl.BlockSpec((1, HALO, CONV_CH),
                         lambda bi, i: (bi, jnp.minimum((i + 1) * hpt, n_tiles * hpt - 1), xbc_block)),
            pl.BlockSpec((8, CONV_CH), lambda bi, i: (0, 0)),
            pl.BlockSpec((1, CONV_CH), lambda bi, i: (0, 0)),
        ],
        out_specs=pl.BlockSpec((1, tc, CONV_CH), lambda bi, i: (bi, i, 0)),
        out_shape=jax.ShapeDtypeStruct((b, l, CONV_CH), BF16),
        scratch_shapes=[pltpu.VMEM((tc + 2 * HALO, CONV_CH), F32)],
        compiler_params=_cparams("parallel", "parallel"),
        name="conv_silu",
    )(proj, proj, proj, conv_w, conv_b)


def _ssd_kernel(xc_ref, dt_ref, cs_ref, dsk_ref, y_ref, st_scr):
    d = pl.program_id(1)
    c = pl.program_id(2)

    @pl.when(c == 0)
    def _():
        st_scr[...] = jnp.zeros_like(st_scr)

    cs = cs_ref[0]
    cs_t = cs.T
    dt_t = dt_ref[0].T
    tot_col = jnp.where(d == 0, cs_t[:, CHUNK - 1:CHUNK], cs_t[:, 0:1])
    tot_b = jnp.broadcast_to(tot_col, (CHUNK, CHUNK))
    wst_t = jnp.exp2(tot_b - cs_t) * dt_t
    cdec = jnp.exp2(tot_b)
    csd_t = cs_t - jnp.log2(dt_t)

    ri = lax.broadcasted_iota(jnp.int32, (CHUNK, CHUNK), 0)
    ci = lax.broadcasted_iota(jnp.int32, (CHUNK, CHUNK), 1)
    causal = jnp.where(d == 0, ri - ci, ci - ri) >= 0
    first = lax.broadcasted_iota(jnp.int32, (1, 2 * SSD_HEAD_DIM), 1) < SSD_HEAD_DIM
    dskip = jnp.where(d == 0, 1.0, 0.0) * dsk_ref[...]

    gn = SSD_GROUPS * SSD_STATE
    pairs = SSD_HEADS // SSD_GROUPS // 2
    for g in range(SSD_GROUPS):
        b_g = xc_ref[0, :, SSD_INNER + g * SSD_STATE:SSD_INNER + (g + 1) * SSD_STATE]
        c_g = xc_ref[0, :, SSD_INNER + gn + g * SSD_STATE:SSD_INNER + gn + (g + 1) * SSD_STATE]
        gmat = _dot_nt(c_g, b_g)
        c_f = c_g.astype(F32)
        b_t = b_g.astype(F32).T
        for j in range(pairs):
            h0 = (g * pairs + j) * 2
            ps = slice(h0 * SSD_HEAD_DIM, (h0 + 2) * SSD_HEAD_DIM)
            x2 = xc_ref[0, :, ps]
            s2 = st_scr[:, ps]
            s2_bf = s2.astype(BF16)
            zero = jnp.zeros_like(x2)
            lhs, bw, rhs, xs = [], [], [], []
            for e in range(2):
                h = h0 + e
                col = jnp.broadcast_to(cs[:, h:h + 1], (CHUNK, CHUNK))
                lmat_dt = jnp.exp2(jnp.where(causal, col - csd_t[h:h + 1, :], NEG))
                lhs.append((gmat * lmat_dt).astype(BF16))
                lhs.append((c_f * jnp.exp2(col)).astype(BF16))
                bw.append((b_t * wst_t[h:h + 1, :]).astype(BF16))
                keep = first if e == 0 else jnp.logical_not(first)
                xe = jnp.where(keep, x2, zero)
                xs.append(xe)
                rhs.append(xe)
                rhs.append(jnp.where(keep, s2_bf, zero))
            y = _dot(jnp.concatenate(lhs, axis=1), jnp.concatenate(rhs, axis=0))
            y = y + dskip[:, ps] * x2.astype(F32)
            y_ref[0, 0, :, ps] = y.astype(y_ref.dtype)
            s_new = _dot(jnp.concatenate(bw, axis=1), jnp.concatenate(xs, axis=0))
            cd = jnp.where(first, cdec[h0:h0 + 1, :], cdec[h0 + 1:h0 + 2, :])
            st_scr[:, ps] = s2 * cd + s_new


def _ssd(xc, dt, cs, d_skip):
    b, l, _ = xc.shape
    nc = l // CHUNK

    def chunk(d, c):
        return jnp.where(d == 0, c, nc - 1 - c)

    return pl.pallas_call(
        _ssd_kernel,
        grid=(b, 2, nc),
        in_specs=[
            pl.BlockSpec((1, CHUNK, CONV_CH), lambda i, d, c: (i, chunk(d, c), 0)),
            pl.BlockSpec((1, CHUNK, LANES), lambda i, d, c: (i, chunk(d, c), d)),
            pl.BlockSpec((1, CHUNK, LANES), lambda i, d, c: (i, chunk(d, c), d)),
            pl.BlockSpec((1, SSD_INNER), lambda i, d, c: (0, 0)),
        ],
        out_specs=pl.BlockSpec((1, 1, CHUNK, SSD_INNER), lambda i, d, c: (d, i, chunk(d, c), 0)),
        out_shape=jax.ShapeDtypeStruct((2, b, l, SSD_INNER), BF16),
        scratch_shapes=[pltpu.VMEM((SSD_STATE, SSD_INNER), F32)],
        compiler_params=_cparams("parallel", "arbitrary", "arbitrary"),
        name="ssd_scan",
    )(xc, dt, cs, d_skip)


NA_UNROLL = 16


def _na_kernel(q_ref, k_ref, v_ref, strip_ref, o_ref, *, rows):
    win = NA_KR * GRID_W
    c1 = (NA_HEAD_DIM ** -0.5) * LOG2E

    def body(it, carry):
        q0s, k0s, scores = [], [], []
        for u in range(NA_UNROLL):
            r = it * NA_UNROLL + u
            rs = jnp.clip(r - NA_KR // 2, 0, rows - NA_KR)
            q0 = pl.multiple_of(r * GRID_W, GRID_W)
            k0 = pl.multiple_of(rs * GRID_W, GRID_W)
            q0s.append(q0)
            k0s.append(k0)
            scores.append(_dot_nt(q_ref[0, pl.ds(q0, GRID_W), :], k_ref[0, pl.ds(k0, win), :]) * c1
                          + strip_ref[0, rs - r + NA_KR - 1])
        probs, dens = [], []
        for s in scores:
            m = jnp.max(s, axis=-1, keepdims=True)
            p = jnp.exp2(s - m)
            dens.append(jnp.sum(p, axis=-1, keepdims=True))
            probs.append(p.astype(BF16))
        for u in range(NA_UNROLL):
            o = _dot(probs[u], v_ref[0, pl.ds(k0s[u], win), :]) / dens[u]
            o_ref[0, pl.ds(q0s[u], GRID_W), :] = o.astype(o_ref.dtype)
        return carry

    lax.fori_loop(0, rows // NA_UNROLL, body, 0)


def _na(proj, strips, q_block, k_block, v_block):
    b, l, _ = proj.shape
    rows = l // GRID_W
    win = NA_KR * GRID_W
    kern = functools.partial(_na_kernel, rows=rows)
    return pl.pallas_call(
        kern,
        grid=(b, NA_HEADS),
        in_specs=[
            pl.BlockSpec((1, l, NA_HEAD_DIM), lambda bi, h: (bi, 0, q_block + h)),
            pl.BlockSpec((1, l, NA_HEAD_DIM), lambda bi, h: (bi, 0, k_block + h)),
            pl.BlockSpec((1, l, NA_HEAD_DIM), lambda bi, h: (bi, 0, v_block + h)),
            pl.BlockSpec((1, NA_KR, GRID_W, win), lambda bi, h: (h, 0, 0, 0)),
        ],
        out_specs=pl.BlockSpec((1, l, NA_HEAD_DIM), lambda bi, h: (bi, 0, h)),
        out_shape=jax.ShapeDtypeStruct((b, l, NA_HEADS * NA_HEAD_DIM), BF16),
        compiler_params=_cparams("parallel", "parallel"),
        name="neighbourhood_attn",
    )(proj, proj, proj, strips)


def _na_strips(rpb):
    qc = np.arange(GRID_W)[:, None]
    kc = np.arange(GRID_W)[None, :]
    cstart = np.clip(qc - NA_KC // 2, 0, GRID_W - NA_KC)
    valid = (kc >= cstart) & (kc < cstart + NA_KC)
    dcidx = np.clip(kc - qc + NA_KC - 1, 0, 2 * NA_KC - 2)
    tb = jnp.where(valid[None, None], rpb[:, :, dcidx].astype(F32) * LOG2E, NEG)
    return jnp.stack(
        [jnp.concatenate([tb[:, o + j] for j in range(NA_KR)], axis=-1) for o in range(NA_KR)], axis=1)


def _merge_kernel(yf_ref, yb_ref, z_ref, yna_ref, gs_ref, gn_ref, gso_ref, bg_ref, w1_ref, w2_ref,
                  o_ref, u_scr):
    @pl.when(pl.program_id(1) == 0)
    def _():
        y = yf_ref[0].astype(F32) + yb_ref[0].astype(F32)
        zf = z_ref[...].astype(F32)
        u_scr[...] = _rms(y * (zf * jax.nn.sigmoid(zf)), gso_ref[...]).astype(BF16)

    g_ssd = jax.nn.sigmoid(gs_ref[...].astype(F32) + bg_ref[0:1, :])
    g_na = jax.nn.sigmoid(gn_ref[...].astype(F32) + bg_ref[1:2, :])
    o_ref[...] = (g_ssd * _dot(u_scr[...], w1_ref[...])
                  + g_na * _dot(yna_ref[...], w2_ref[...])).astype(o_ref.dtype)


def _merge(y2, proj, yna, g_ssd_out, b_gate, w1, w2, z_block, gs_block, gn_block, tm, tn):
    _, m, d = y2.shape
    r = d // tn
    return pl.pallas_call(
        _merge_kernel,
        grid=(m // tm, d // tn),
        in_specs=[
            pl.BlockSpec((1, tm, d), lambda i, j: (0, i, 0)),
            pl.BlockSpec((1, tm, d), lambda i, j: (1, i, 0)),
            pl.BlockSpec((tm, d), lambda i, j: (i, z_block)),
            pl.BlockSpec((tm, d), lambda i, j: (i, 0)),
            pl.BlockSpec((tm, tn), lambda i, j: (i, gs_block * r + j)),
            pl.BlockSpec((tm, tn), lambda i, j: (i, gn_block * r + j)),
            pl.BlockSpec((1, d), lambda i, j: (0, 0)),
            pl.BlockSpec((2, tn), lambda i, j: (0, j)),
            pl.BlockSpec((d, tn), lambda i, j: (0, j)),
            pl.BlockSpec((d, tn), lambda i, j: (0, j)),
        ],
        out_specs=pl.BlockSpec((tm, tn), lambda i, j: (i, j)),
        out_shape=jax.ShapeDtypeStruct((m, d), BF16),
        scratch_shapes=[pltpu.VMEM((tm, d), BF16)],
        compiler_params=_cparams("parallel", "arbitrary"),
        name="branch_merge",
    )(y2, y2, proj, yna, proj, proj, g_ssd_out, b_gate, w1, w2)


def _proj_norm_res_kernel(a_ref, w_ref, g_ref, x_ref, o_ref):
    o_ref[...] = x_ref[...] + _rms(_dot(a_ref[...], w_ref[...]), g_ref[...])


def _proj_norm_res(a, w, g, x, tm):
    m, k = a.shape
    d = w.shape[1]
    return pl.pallas_call(
        _proj_norm_res_kernel,
        grid=(m // tm,),
        in_specs=[
            pl.BlockSpec((tm, k), lambda i: (i, 0)),
            pl.BlockSpec((k, d), lambda i: (0, 0)),
            pl.BlockSpec((1, d), lambda i: (0, 0)),
            pl.BlockSpec((tm, d), lambda i: (i, 0)),
        ],
        out_specs=pl.BlockSpec((tm, d), lambda i: (i, 0)),
        out_shape=jax.ShapeDtypeStruct((m, d), F32),
        compiler_params=_cparams("parallel"),
        name="proj_norm_residual",
    )(a, w, g, x)


def _xattn_kernel(x_ref, kv_ref, gpre_ref, wq_ref, wo_ref, gpost_ref, o_ref):
    x = x_ref[0]
    a = _rms(x, gpre_ref[...]).astype(BF16)
    q = _dot(a, wq_ref[...]).astype(BF16)
    width = CA_HEADS * CA_HEAD_DIM
    outs = []
    for h in range(CA_HEADS):
        hs = slice(h * CA_HEAD_DIM, (h + 1) * CA_HEAD_DIM)
        kh = kv_ref[0, :, hs]
        vh = kv_ref[0, :, width + h * CA_HEAD_DIM:width + (h + 1) * CA_HEAD_DIM]
        s = _dot_nt(q[:, hs], kh) * (CA_HEAD_DIM ** -0.5)
        m = jnp.max(s, axis=-1, keepdims=True)
        p = jnp.exp(s - m)
        den = jnp.sum(p, axis=-1, keepdims=True)
        outs.append((_dot(p.astype(BF16), vh) / den).astype(BF16))
    o = jnp.concatenate(outs, axis=1)
    o_ref[0] = x + _rms(_dot(o, wo_ref[...]), gpost_ref[...])


def _xattn(x, kv, g_pre, wq, wo, g_post, tm):
    b, l, d = x.shape
    n_mem, kvw = kv.shape[1], kv.shape[2]
    width = wq.shape[1]
    return pl.pallas_call(
        _xattn_kernel,
        grid=(b, l // tm),
        in_specs=[
            pl.BlockSpec((1, tm, d), lambda bi, i: (bi, i, 0)),
            pl.BlockSpec((1, n_mem, kvw), lambda bi, i: (bi, 0, 0)),
            pl.BlockSpec((1, d), lambda bi, i: (0, 0)),
            pl.BlockSpec((d, width), lambda bi, i: (0, 0)),
            pl.BlockSpec((width, d), lambda bi, i: (0, 0)),
            pl.BlockSpec((1, d), lambda bi, i: (0, 0)),
        ],
        out_specs=pl.BlockSpec((1, tm, d), lambda bi, i: (bi, i, 0)),
        out_shape=jax.ShapeDtypeStruct((b, l, d), F32),
        compiler_params=_cparams("parallel", "parallel"),
        name="mem_xattn",
    )(x, kv, g_pre, wq, wo, g_post)


def _ffn_kernel(x_ref, gpre_ref, wg_ref, wu_ref, wd_ref, gpost_ref, o_ref, a_scr, acc_scr):
    f = pl.program_id(1)

    @pl.when(f == 0)
    def _():
        a_scr[...] = _rms(x_ref[...], gpre_ref[...]).astype(BF16)
        acc_scr[...] = jnp.zeros_like(acc_scr)

    a = a_scr[...]
    hg = _dot(a, wg_ref[...])
    hu = _dot(a, wu_ref[...])
    act = (hg * jax.nn.sigmoid(hg) * hu).astype(BF16)
    acc_scr[...] += _dot(act, wd_ref[...])

    @pl.when(f == pl.num_programs(1) - 1)
    def _():
        o_ref[...] = x_ref[...] + _rms(acc_scr[...], gpost_ref[...])


def _ffn(x, g_pre, w_gu, w_down, g_post, tm, tf):
    m, d = x.shape
    dff = w_down.shape[0]
    nf = dff // tf
    return pl.pallas_call(
        _ffn_kernel,
        grid=(m // tm, nf),
        in_specs=[
            pl.BlockSpec((tm, d), lambda i, f: (i, 0)),
            pl.BlockSpec((1, d), lambda i, f: (0, 0)),
            pl.BlockSpec((d, tf), lambda i, f: (0, f)),
            pl.BlockSpec((d, tf), lambda i, f: (0, nf + f)),
            pl.BlockSpec((tf, d), lambda i, f: (f, 0)),
            pl.BlockSpec((1, d), lambda i, f: (0, 0)),
        ],
        out_specs=pl.BlockSpec((tm, d), lambda i, f: (i, 0)),
        out_shape=jax.ShapeDtypeStruct((m, d), F32),
        scratch_shapes=[pltpu.VMEM((tm, d), BF16), pltpu.VMEM((tm, d), F32)],
        compiler_params=_cparams("parallel", "arbitrary"),
        name="swiglu_ffn",
    )(x, g_pre, w_gu, w_gu, w_down, g_post)


def _pick(n, target):
    t = min(n, target)
    while n % t:
        t //= 2
    return t


def _prepare(w_in, b_gate, conv_w, conv_b, a_log, dt_bias, d_skip, g_ssd_out, rpb,
             w_br_ssd, w_br_na, w_out, g_mix_pre, g_mix_post, g_ca_pre, g_mem, w_ca_q, w_ca_kv,
             w_ca_o, g_ca_post, g_ffn_pre, w_ffn_gu, w_ffn_down, g_ffn_post):
    d = w_in.shape[0]
    row = lambda v: v.reshape(1, -1).astype(F32)
    o1 = SSD_INNER
    o2 = o1 + CONV_CH
    o3 = o2 + 2 * SSD_HEADS
    lane_pad = jnp.zeros((d, LANES - SSD_HEADS), w_in.dtype)
    head_pad = jnp.zeros((2, LANES - SSD_HEADS), F32)
    lanes2 = lambda v: jnp.concatenate([v.astype(F32).reshape(2, SSD_HEADS), head_pad], axis=1).reshape(1, -1)
    return dict(
        w_main=jnp.concatenate([w_in[:, :o1], w_in[:, o3:], w_in[:, o1:o2]], axis=1).astype(BF16),
        w_dt=jnp.concatenate([w_in[:, o2:o2 + SSD_HEADS], lane_pad, w_in[:, o2 + SSD_HEADS:o3], lane_pad],
                             axis=1).astype(BF16),
        a_log=lanes2(a_log), dt_bias=lanes2(dt_bias),
        conv_w=jnp.pad(conv_w.astype(F32), ((0, 8 - SSD_CONV), (0, 0))), conv_b=row(conv_b),
        d_skip=row(jnp.repeat(d_skip, SSD_HEAD_DIM)), g_ssd_out=row(g_ssd_out),
        strips=_na_strips(rpb), b_gate=b_gate.astype(F32),
        w_br_ssd=w_br_ssd.astype(BF16), w_br_na=w_br_na.astype(BF16), w_out=w_out.astype(BF16),
        g_mix_pre=row(g_mix_pre), g_mix_post=row(g_mix_post), g_ca_pre=row(g_ca_pre), g_mem=row(g_mem),
        w_ca_q=w_ca_q.astype(BF16), w_ca_kv=w_ca_kv.astype(BF16), w_ca_o=w_ca_o.astype(BF16),
        g_ca_post=row(g_ca_post), g_ffn_pre=row(g_ffn_pre), w_ffn_gu=w_ffn_gu.astype(BF16),
        w_ffn_down=w_ffn_down.astype(BF16), g_ffn_post=row(g_ffn_post))


def _layer(x, mem, p):
    b, l, d = x.shape
    m = b * l
    na_w = NA_HEADS * NA_HEAD_DIM
    z_block, q_col, k_col, v_col = 0, d, d + na_w, d + 2 * na_w
    gs_col = d + 3 * na_w
    gn_col = gs_col + d
    xbc_col = gn_col + d
    assert xbc_col % CONV_CH == 0 and gs_col % d == 0 and gn_col % d == 0

    x2d = x.reshape(m, d)
    proj = _norm_matmul(x2d, p["g_mix_pre"], p["w_main"], BF16, _pick(m, 1024), 1024)
    dt, cs = _dt_scan(x2d, p["g_mix_pre"], p["w_dt"], p["a_log"], p["dt_bias"], _pick(l, 512))

    proj3 = proj.reshape(b, l, -1)
    xc = _conv_silu(proj3, p["conv_w"], p["conv_b"], xbc_col // CONV_CH, _pick(l, 512))
    y2 = _ssd(xc, dt.reshape(b, l, -1), cs.reshape(b, l, -1), p["d_skip"])

    y_na = _na(proj3, p["strips"], q_col // NA_HEAD_DIM, k_col // NA_HEAD_DIM, v_col // NA_HEAD_DIM)

    merged = _merge(y2.reshape(2, m, SSD_INNER), proj, y_na.reshape(m, na_w), p["g_ssd_out"],
                    p["b_gate"], p["w_br_ssd"], p["w_br_na"],
                    z_block, gs_col // d, gn_col // d, _pick(m, 256), 1024)
    x1 = _proj_norm_res(merged, p["w_out"], p["g_mix_post"], x2d, _pick(m, 256))

    n_mem = mem.shape[1]
    kv = _norm_matmul(mem.reshape(b * n_mem, d), p["g_mem"], p["w_ca_kv"], BF16, _pick(b * n_mem, 256), 1024)
    x2 = _xattn(x1.reshape(b, l, d), kv.reshape(b, n_mem, -1), p["g_ca_pre"], p["w_ca_q"],
                p["w_ca_o"], p["g_ca_post"], _pick(l, 512))

    x3 = _ffn(x2.reshape(m, d), p["g_ffn_pre"], p["w_ffn_gu"], p["w_ffn_down"], p["g_ffn_post"],
              _pick(m, 512), 512)
    return x3.reshape(b, l, d)


def kernel(x_prompt, x_sample, mem_prompt, mem_sample, w_in, b_gate, conv_w, conv_b, a_log, dt_bias, d_skip, g_ssd_out, rpb, w_br_ssd, w_br_na, w_out, g_mix_pre, g_mix_post, g_ca_pre, g_mem, w_ca_q, w_ca_kv, w_ca_o, g_ca_post, g_ffn_pre, w_ffn_gu, w_ffn_down, g_ffn_post):
    params = (w_in, b_gate, conv_w, conv_b, a_log, dt_bias, d_skip, g_ssd_out, rpb, w_br_ssd,
              w_br_na, w_out, g_mix_pre, g_mix_post, g_ca_pre, g_mem, w_ca_q, w_ca_kv, w_ca_o,
              g_ca_post, g_ffn_pre, w_ffn_gu, w_ffn_down, g_ffn_post)
    y_prompt, y_sample = x_prompt, x_sample
    for i in range(w_in.shape[0]):
        p = _prepare(*[w[i] for w in params])
        y_prompt = _layer(y_prompt, mem_prompt, p)
        y_sample = _layer(y_sample, mem_sample, p)
    return (y_prompt, y_sample)
```

```python
import functools
import math

import numpy as np
import jax
import jax.numpy as jnp
from jax import lax
from jax.experimental import pallas as pl
from jax.experimental.pallas import tpu as pltpu

BF16 = jnp.bfloat16
F32 = jnp.float32

EPS = 1e-6
LANES = 128
GRID_W = 64
SSD_HEADS = 32
SSD_HEAD_DIM = 64
SSD_GROUPS = 4
SSD_STATE = 128
SSD_INNER = SSD_HEADS * SSD_HEAD_DIM
SSD_CONV = 5
CONV_CH = SSD_INNER + 2 * SSD_GROUPS * SSD_STATE
CHUNK = 128
HALO = 16
NA_HEADS = 16
NA_HEAD_DIM = 128
NA_KR = 8
NA_KC = 16
CA_HEADS = 4
CA_HEAD_DIM = 128
NEG = -1e30
LOG2E = math.log2(math.e)

VMEM_LIMIT_BYTES = 56 * 1024 * 1024


def _cparams(*sem):
    return pltpu.CompilerParams(dimension_semantics=sem, vmem_limit_bytes=VMEM_LIMIT_BYTES)


def _rms(x, g):
    return x * lax.rsqrt(jnp.mean(x * x, axis=-1, keepdims=True) + EPS) * g


def _dot(a, b):
    return jnp.dot(a, b, preferred_element_type=F32)


def _dot_nt(a, b):
    return lax.dot_general(a, b, (((1,), (1,)), ((), ())), preferred_element_type=F32)


def _norm_matmul_kernel(x_ref, g_ref, w_ref, o_ref, a_scr):
    @pl.when(pl.program_id(1) == 0)
    def _():
        a_scr[...] = _rms(x_ref[...], g_ref[...]).astype(BF16)

    o_ref[...] = _dot(a_scr[...], w_ref[...]).astype(o_ref.dtype)


def _norm_matmul(x, g, w, out_dtype, tm, tn):
    m, d = x.shape
    n = w.shape[1]
    return pl.pallas_call(
        _norm_matmul_kernel,
        grid=(m // tm, n // tn),
        in_specs=[
            pl.BlockSpec((tm, d), lambda i, j: (i, 0)),
            pl.BlockSpec((1, d), lambda i, j: (0, 0)),
            pl.BlockSpec((d, tn), lambda i, j: (0, j)),
        ],
        out_specs=pl.BlockSpec((tm, tn), lambda i, j: (i, j)),
        out_shape=jax.ShapeDtypeStruct((m, n), out_dtype),
        scratch_shapes=[pltpu.VMEM((tm, d), BF16)],
        compiler_params=_cparams("parallel", "arbitrary"),
        name="norm_matmul",
    )(x, g, w)


def _split3(x):
    hi = x.astype(BF16)
    r1 = x - hi.astype(F32)
    mid = r1.astype(BF16)
    lo = (r1 - mid.astype(F32)).astype(BF16)
    return hi, mid, lo


def _dt_kernel(x_ref, g_ref, w_ref, alog_ref, dtb_ref, dt_ref, cs_ref):
    tm = x_ref.shape[0]
    a = _rms(x_ref[...], g_ref[...]).astype(BF16)
    z = _dot(a, w_ref[...]) + dtb_ref[...]
    dt = jnp.maximum(z, 0.0) + jnp.log1p(jnp.exp(-jnp.abs(z)))
    da = dt * -jnp.exp(alog_ref[...])
    ri = lax.broadcasted_iota(jnp.int32, (tm, tm), 0)
    ci = lax.broadcasted_iota(jnp.int32, (tm, tm), 1)
    same = (ri // CHUNK) == (ci // CHUNK)
    t_fwd = (same & (ri >= ci)).astype(BF16)
    t_bwd = (same & (ri <= ci)).astype(BF16)
    hi, mid, lo = _split3(da)
    cs_f = _dot(t_fwd, hi[:, :LANES]) + _dot(t_fwd, mid[:, :LANES]) + _dot(t_fwd, lo[:, :LANES])
    cs_b = _dot(t_bwd, hi[:, LANES:]) + _dot(t_bwd, mid[:, LANES:]) + _dot(t_bwd, lo[:, LANES:])
    dt_ref[...] = dt
    cs_ref[:, :LANES] = cs_f * LOG2E
    cs_ref[:, LANES:] = cs_b * LOG2E


def _dt_scan(x, g, w_dt, a_log, dt_bias, tm):
    m, d = x.shape
    n = 2 * LANES
    shp = jax.ShapeDtypeStruct((m, n), F32)
    return pl.pallas_call(
        _dt_kernel,
        grid=(m // tm,),
        in_specs=[
            pl.BlockSpec((tm, d), lambda i: (i, 0)),
            pl.BlockSpec((1, d), lambda i: (0, 0)),
            pl.BlockSpec((d, n), lambda i: (0, 0)),
            pl.BlockSpec((1, n), lambda i: (0, 0)),
            pl.BlockSpec((1, n), lambda i: (0, 0)),
        ],
        out_specs=[pl.BlockSpec((tm, n), lambda i: (i, 0)), pl.BlockSpec((tm, n), lambda i: (i, 0))],
        out_shape=[shp, shp],
        compiler_params=_cparams("parallel"),
        name="ssd_dt_scan",
    )(x, g, w_dt, a_log, dt_bias)


def _conv_kernel(xm_ref, xp_ref, xn_ref, cw_ref, cb_ref, o_ref, xs_scr, *, n_tiles):
    i = pl.program_id(1)
    tc = xm_ref.shape[1]
    has_prev = (i > 0).astype(F32)
    has_next = (i < n_tiles - 1).astype(F32)
    xs_scr[0:HALO, :] = xp_ref[0].astype(F32) * has_prev
    xs_scr[HALO:HALO + tc, :] = xm_ref[0].astype(F32)
    xs_scr[HALO + tc:, :] = xn_ref[0].astype(F32) * has_next
    for r in range(tc // CHUNK):
        acc = jnp.broadcast_to(cb_ref[...], (CHUNK, CONV_CH))
        for k in range(SSD_CONV):
            off = r * CHUNK + HALO - SSD_CONV // 2 + k
            acc = acc + cw_ref[k:k + 1, :] * xs_scr[off:off + CHUNK, :]
        o_ref[0, r * CHUNK:(r + 1) * CHUNK, :] = (acc * jax.nn.sigmoid(acc)).astype(o_ref.dtype)


def _conv_silu(proj, conv_w, conv_b, xbc_block, tc):
    b, l, _ = proj.shape
    n_tiles = l // tc
    hpt = tc // HALO
    kern = functools.partial(_conv_kernel, n_tiles=n_tiles)
    return pl.pallas_call(
        kern,
        grid=(b, n_tiles),
        in_specs=[
            pl.BlockSpec((1, tc, CONV_CH), lambda bi, i: (bi, i, xbc_block)),
            pl.BlockSpec((1, HALO, CONV_CH), lambda bi, i: (bi, jnp.maximum(i * hpt - 1, 0), xbc_block)),
            pl.BlockSpec((1, HALO, CONV_CH),
                         lambda bi, i: (bi, jnp.minimum((i + 1) * hpt, n_tiles * hpt - 1), xbc_block)),
            pl.BlockSpec((8, CONV_CH), lambda bi, i: (0, 0)),
            pl.BlockSpec((1, CONV_CH), lambda bi, i: (0, 0)),
        ],
        out_specs=pl.BlockSpec((1, tc, CONV_CH), lambda bi, i: (bi, i, 0)),
        out_shape=jax.ShapeDtypeStruct((b, l, CONV_CH), BF16),
        scratch_shapes=[pltpu.VMEM((tc + 2 * HALO, CONV_CH), F32)],
        compiler_params=_cparams("parallel", "parallel"),
        name="conv_silu",
    )(proj, proj, proj, conv_w, conv_b)


def _ssd_kernel(xc_ref, dt_ref, cs_ref, dsk_ref, y_ref, st_scr):
    d = pl.program_id(1)
    c = pl.program_id(2)

    @pl.when(c == 0)
    def _():
        st_scr[...] = jnp.zeros_like(st_scr)

    cs = cs_ref[0]
    cs_t = cs.T
    dt_t = dt_ref[0].T
    tot_col = jnp.where(d == 0, cs_t[:, CHUNK - 1:CHUNK], cs_t[:, 0:1])
    tot_b = jnp.broadcast_to(tot_col, (CHUNK, CHUNK))
    wst_t = jnp.exp2(tot_b - cs_t) * dt_t
    cdec = jnp.exp2(tot_b)
    csd_t = cs_t - jnp.log2(dt_t)

    ri = lax.broadcasted_iota(jnp.int32, (CHUNK, CHUNK), 0)
    ci = lax.broadcasted_iota(jnp.int32, (CHUNK, CHUNK), 1)
    causal = jnp.where(d == 0, ri - ci, ci - ri) >= 0
    first = lax.broadcasted_iota(jnp.int32, (1, 2 * SSD_HEAD_DIM), 1) < SSD_HEAD_DIM
    dskip = jnp.where(d == 0, 1.0, 0.0) * dsk_ref[...]

    gn = SSD_GROUPS * SSD_STATE
    pairs = SSD_HEADS // SSD_GROUPS // 2
    for g in range(SSD_GROUPS):
        b_g = xc_ref[0, :, SSD_INNER + g * SSD_STATE:SSD_INNER + (g + 1) * SSD_STATE]
        c_g = xc_ref[0, :, SSD_INNER + gn + g * SSD_STATE:SSD_INNER + gn + (g + 1) * SSD_STATE]
        gmat = _dot_nt(c_g, b_g)
        c_f = c_g.astype(F32)
        b_t = b_g.astype(F32).T
        for j in range(pairs):
            h0 = (g * pairs + j) * 2
            ps = slice(h0 * SSD_HEAD_DIM, (h0 + 2) * SSD_HEAD_DIM)
            x2 = xc_ref[0, :, ps]
            s2 = st_scr[:, ps]
            s2_bf = s2.astype(BF16)
            zero = jnp.zeros_like(x2)
            lhs, bw, rhs, xs = [], [], [], []
            for e in range(2):
                h = h0 + e
                col = jnp.broadcast_to(cs[:, h:h + 1], (CHUNK, CHUNK))
                lmat_dt = jnp.exp2(jnp.where(causal, col - csd_t[h:h + 1, :], NEG))
                lhs.append((gmat * lmat_dt).astype(BF16))
                lhs.append((c_f * jnp.exp2(col)).astype(BF16))
                bw.append((b_t * wst_t[h:h + 1, :]).astype(BF16))
                keep = first if e == 0 else jnp.logical_not(first)
                xe = jnp.where(keep, x2, zero)
                xs.append(xe)
                rhs.append(xe)
                rhs.append(jnp.where(keep, s2_bf, zero))
            y = _dot(jnp.concatenate(lhs, axis=1), jnp.concatenate(rhs, axis=0))
            y = y + dskip[:, ps] * x2.astype(F32)
            y_ref[0, 0, :, ps] = y.astype(y_ref.dtype)
            s_new = _dot(jnp.concatenate(bw, axis=1), jnp.concatenate(xs, axis=0))
            cd = jnp.where(first, cdec[h0:h0 + 1, :], cdec[h0 + 1:h0 + 2, :])
            st_scr[:, ps] = s2 * cd + s_new


def _ssd(xc, dt, cs, d_skip):
    b, l, _ = xc.shape
    nc = l // CHUNK

    def chunk(d, c):
        return jnp.where(d == 0, c, nc - 1 - c)

    return pl.pallas_call(
        _ssd_kernel,
        grid=(b, 2, nc),
        in_specs=[
            pl.BlockSpec((1, CHUNK, CONV_CH), lambda i, d, c: (i, chunk(d, c), 0)),
            pl.BlockSpec((1, CHUNK, LANES), lambda i, d, c: (i, chunk(d, c), d)),
            pl.BlockSpec((1, CHUNK, LANES), lambda i, d, c: (i, chunk(d, c), d)),
            pl.BlockSpec((1, SSD_INNER), lambda i, d, c: (0, 0)),
        ],
        out_specs=pl.BlockSpec((1, 1, CHUNK, SSD_INNER), lambda i, d, c: (d, i, chunk(d, c), 0)),
        out_shape=jax.ShapeDtypeStruct((2, b, l, SSD_INNER), BF16),
        scratch_shapes=[pltpu.VMEM((SSD_STATE, SSD_INNER), F32)],
        compiler_params=_cparams("parallel", "arbitrary", "arbitrary"),
        name="ssd_scan",
    )(xc, dt, cs, d_skip)


NA_UNROLL = 16
NA_HEADS_PER_STEP = 2


def _na_kernel(q_ref, k_ref, v_ref, strip_ref, o_ref, *, rows):
    win = NA_KR * GRID_W
    c1 = (NA_HEAD_DIM ** -0.5) * LOG2E

    def body(hh, it, carry):
        hs = slice(hh * NA_HEAD_DIM, (hh + 1) * NA_HEAD_DIM)
        q0s, k0s, scores = [], [], []
        for u in range(NA_UNROLL):
            r = it * NA_UNROLL + u
            rs = jnp.clip(r - NA_KR // 2, 0, rows - NA_KR)
            q0 = pl.multiple_of(r * GRID_W, GRID_W)
            k0 = pl.multiple_of(rs * GRID_W, GRID_W)
            q0s.append(q0)
            k0s.append(k0)
            scores.append(_dot_nt(q_ref[0, pl.ds(q0, GRID_W), hs], k_ref[0, pl.ds(k0, win), hs]) * c1
                          + strip_ref[hh, rs - r + NA_KR - 1])
        probs, dens = [], []
        for s in scores:
            m = jnp.max(s, axis=-1, keepdims=True)
            p = jnp.exp2(s - m)
            dens.append(jnp.sum(p, axis=-1, keepdims=True))
            probs.append(p.astype(BF16))
        for u in range(NA_UNROLL):
            o = _dot(probs[u], v_ref[0, pl.ds(k0s[u], win), hs]) / dens[u]
            o_ref[0, pl.ds(q0s[u], GRID_W), hs] = o.astype(o_ref.dtype)
        return carry

    for hh in range(NA_HEADS_PER_STEP):
        lax.fori_loop(0, rows // NA_UNROLL, functools.partial(body, hh), 0)


def _na(proj, strips, q_block, k_block, v_block):
    b, l, _ = proj.shape
    rows = l // GRID_W
    win = NA_KR * GRID_W
    hps = NA_HEADS_PER_STEP
    wide = hps * NA_HEAD_DIM
    assert q_block % hps == 0 and k_block % hps == 0 and v_block % hps == 0
    kern = functools.partial(_na_kernel, rows=rows)
    return pl.pallas_call(
        kern,
        grid=(b, NA_HEADS // hps),
        in_specs=[
            pl.BlockSpec((1, l, wide), lambda bi, h: (bi, 0, q_block // hps + h)),
            pl.BlockSpec((1, l, wide), lambda bi, h: (bi, 0, k_block // hps + h)),
            pl.BlockSpec((1, l, wide), lambda bi, h: (bi, 0, v_block // hps + h)),
            pl.BlockSpec((hps, NA_KR, GRID_W, win), lambda bi, h: (h, 0, 0, 0)),
        ],
        out_specs=pl.BlockSpec((1, l, wide), lambda bi, h: (bi, 0, h)),
        out_shape=jax.ShapeDtypeStruct((b, l, NA_HEADS * NA_HEAD_DIM), BF16),
        compiler_params=_cparams("parallel", "parallel"),
        name="neighbourhood_attn",
    )(proj, proj, proj, strips)


def _na_strips(rpb):
    qc = np.arange(GRID_W)[:, None]
    kc = np.arange(GRID_W)[None, :]
    cstart = np.clip(qc - NA_KC // 2, 0, GRID_W - NA_KC)
    valid = (kc >= cstart) & (kc < cstart + NA_KC)
    dcidx = np.clip(kc - qc + NA_KC - 1, 0, 2 * NA_KC - 2)
    tb = jnp.where(valid[None, None], rpb[:, :, dcidx].astype(F32) * LOG2E, NEG)
    return jnp.stack(
        [jnp.concatenate([tb[:, o + j] for j in range(NA_KR)], axis=-1) for o in range(NA_KR)], axis=1)


def _merge_kernel(yf_ref, yb_ref, z_ref, yna_ref, gs_ref, gn_ref, gso_ref, bg_ref, w1_ref, w2_ref, o_ref):
    g_na = jax.nn.sigmoid(gn_ref[...].astype(F32) + bg_ref[1:2, :])
    br_na = g_na * _dot(yna_ref[...], w2_ref[...])
    y = yf_ref[0].astype(F32) + yb_ref[0].astype(F32)
    zf = z_ref[...].astype(F32)
    u = _rms(y * (zf * jax.nn.sigmoid(zf)), gso_ref[...]).astype(BF16)
    g_ssd = jax.nn.sigmoid(gs_ref[...].astype(F32) + bg_ref[0:1, :])
    o_ref[...] = (g_ssd * _dot(u, w1_ref[...]) + br_na).astype(o_ref.dtype)


def _resident(shape):
    return pl.BlockSpec(shape, lambda *_: (0,) * len(shape), pipeline_mode=pl.Buffered(1))


def _merge(y2, proj, yna, g_ssd_out, b_gate, w1, w2, z_block, gs_block, gn_block, tm):
    _, m, d = y2.shape
    return pl.pallas_call(
        _merge_kernel,
        grid=(m // tm,),
        in_specs=[
            pl.BlockSpec((1, tm, d), lambda i: (0, i, 0)),
            pl.BlockSpec((1, tm, d), lambda i: (1, i, 0)),
            pl.BlockSpec((tm, d), lambda i: (i, z_block)),
            pl.BlockSpec((tm, d), lambda i: (i, 0)),
            pl.BlockSpec((tm, d), lambda i: (i, gs_block)),
            pl.BlockSpec((tm, d), lambda i: (i, gn_block)),
            _resident((1, d)),
            _resident((2, d)),
            _resident((d, d)),
            _resident((d, d)),
        ],
        out_specs=pl.BlockSpec((tm, d), lambda i: (i, 0)),
        out_shape=jax.ShapeDtypeStruct((m, d), BF16),
        compiler_params=_cparams("parallel"),
        name="branch_merge",
    )(y2, y2, proj, yna, proj, proj, g_ssd_out, b_gate, w1, w2)


def _proj_norm_res_kernel(a_ref, w_ref, g_ref, x_ref, o_ref):
    o_ref[...] = x_ref[...] + _rms(_dot(a_ref[...], w_ref[...]), g_ref[...])


def _proj_norm_res(a, w, g, x, tm):
    m, k = a.shape
    d = w.shape[1]
    return pl.pallas_call(
        _proj_norm_res_kernel,
        grid=(m // tm,),
        in_specs=[
            pl.BlockSpec((tm, k), lambda i: (i, 0)),
            _resident((k, d)),
            _resident((1, d)),
            pl.BlockSpec((tm, d), lambda i: (i, 0)),
        ],
        out_specs=pl.BlockSpec((tm, d), lambda i: (i, 0)),
        out_shape=jax.ShapeDtypeStruct((m, d), F32),
        compiler_params=_cparams("parallel"),
        name="proj_norm_residual",
    )(a, w, g, x)


def _xattn_kernel(x_ref, kv_ref, gpre_ref, wq_ref, wo_ref, gpost_ref, o_ref):
    x = x_ref[0]
    a = _rms(x, gpre_ref[...]).astype(BF16)
    q = _dot(a, wq_ref[...]).astype(BF16)
    width = CA_HEADS * CA_HEAD_DIM
    outs = []
    for h in range(CA_HEADS):
        hs = slice(h * CA_HEAD_DIM, (h + 1) * CA_HEAD_DIM)
        kh = kv_ref[0, :, hs]
        vh = kv_ref[0, :, width + h * CA_HEAD_DIM:width + (h + 1) * CA_HEAD_DIM]
        s = _dot_nt(q[:, hs], kh) * (CA_HEAD_DIM ** -0.5)
        m = jnp.max(s, axis=-1, keepdims=True)
        p = jnp.exp(s - m)
        den = jnp.sum(p, axis=-1, keepdims=True)
        outs.append((_dot(p.astype(BF16), vh) / den).astype(BF16))
    o = jnp.concatenate(outs, axis=1)
    o_ref[0] = x + _rms(_dot(o, wo_ref[...]), gpost_ref[...])


def _xattn(x, kv, g_pre, wq, wo, g_post, tm):
    b, l, d = x.shape
    n_mem, kvw = kv.shape[1], kv.shape[2]
    width = wq.shape[1]
    return pl.pallas_call(
        _xattn_kernel,
        grid=(b, l // tm),
        in_specs=[
            pl.BlockSpec((1, tm, d), lambda bi, i: (bi, i, 0)),
            pl.BlockSpec((1, n_mem, kvw), lambda bi, i: (bi, 0, 0)),
            pl.BlockSpec((1, d), lambda bi, i: (0, 0)),
            pl.BlockSpec((d, width), lambda bi, i: (0, 0)),
            pl.BlockSpec((width, d), lambda bi, i: (0, 0)),
            pl.BlockSpec((1, d), lambda bi, i: (0, 0)),
        ],
        out_specs=pl.BlockSpec((1, tm, d), lambda bi, i: (bi, i, 0)),
        out_shape=jax.ShapeDtypeStruct((b, l, d), F32),
        compiler_params=_cparams("parallel", "parallel"),
        name="mem_xattn",
    )(x, kv, g_pre, wq, wo, g_post)


def _ffn_kernel(x_ref, gpre_ref, wg_ref, wu_ref, wd_ref, gpost_ref, o_ref, a_scr, acc_scr):
    f = pl.program_id(1)

    @pl.when(f == 0)
    def _():
        a_scr[...] = _rms(x_ref[...], gpre_ref[...]).astype(BF16)
        acc_scr[...] = jnp.zeros_like(acc_scr)

    a = a_scr[...]
    hg = _dot(a, wg_ref[...])
    hu = _dot(a, wu_ref[...])
    act = (hg * jax.nn.sigmoid(hg) * hu).astype(BF16)
    acc_scr[...] += _dot(act, wd_ref[...])

    @pl.when(f == pl.num_programs(1) - 1)
    def _():
        o_ref[...] = x_ref[...] + _rms(acc_scr[...], gpost_ref[...])


def _ffn(x, g_pre, w_gu, w_down, g_post, tm, tf):
    m, d = x.shape
    dff = w_down.shape[0]
    nf = dff // tf
    return pl.pallas_call(
        _ffn_kernel,
        grid=(m // tm, nf),
        in_specs=[
            pl.BlockSpec((tm, d), lambda i, f: (i, 0)),
            pl.BlockSpec((1, d), lambda i, f: (0, 0)),
            pl.BlockSpec((d, tf), lambda i, f: (0, f)),
            pl.BlockSpec((d, tf), lambda i, f: (0, nf + f)),
            pl.BlockSpec((tf, d), lambda i, f: (f, 0)),
            pl.BlockSpec((1, d), lambda i, f: (0, 0)),
        ],
        out_specs=pl.BlockSpec((tm, d), lambda i, f: (i, 0)),
        out_shape=jax.ShapeDtypeStruct((m, d), F32),
        scratch_shapes=[pltpu.VMEM((tm, d), BF16), pltpu.VMEM((tm, d), F32)],
        compiler_params=_cparams("parallel", "arbitrary"),
        name="swiglu_ffn",
    )(x, g_pre, w_gu, w_gu, w_down, g_post)


def _pick(n, target):
    t = min(n, target)
    while n % t:
        t //= 2
    return t


def _prepare(w_in, b_gate, conv_w, conv_b, a_log, dt_bias, d_skip, g_ssd_out, rpb,
             w_br_ssd, w_br_na, w_out, g_mix_pre, g_mix_post, g_ca_pre, g_mem, w_ca_q, w_ca_kv,
             w_ca_o, g_ca_post, g_ffn_pre, w_ffn_gu, w_ffn_down, g_ffn_post):
    d = w_in.shape[0]
    row = lambda v: v.reshape(1, -1).astype(F32)
    o1 = SSD_INNER
    o2 = o1 + CONV_CH
    o3 = o2 + 2 * SSD_HEADS
    lane_pad = jnp.zeros((d, LANES - SSD_HEADS), w_in.dtype)
    head_pad = jnp.zeros((2, LANES - SSD_HEADS), F32)
    lanes2 = lambda v: jnp.concatenate([v.astype(F32).reshape(2, SSD_HEADS), head_pad], axis=1).reshape(1, -1)
    return dict(
        w_main=jnp.concatenate([w_in[:, :o1], w_in[:, o3:], w_in[:, o1:o2]], axis=1).astype(BF16),
        w_dt=jnp.concatenate([w_in[:, o2:o2 + SSD_HEADS], lane_pad, w_in[:, o2 + SSD_HEADS:o3], lane_pad],
                             axis=1).astype(BF16),
        a_log=lanes2(a_log), dt_bias=lanes2(dt_bias),
        conv_w=jnp.pad(conv_w.astype(F32), ((0, 8 - SSD_CONV), (0, 0))), conv_b=row(conv_b),
        d_skip=row(jnp.repeat(d_skip, SSD_HEAD_DIM)), g_ssd_out=row(g_ssd_out),
        strips=_na_strips(rpb), b_gate=b_gate.astype(F32),
        w_br_ssd=w_br_ssd.astype(BF16), w_br_na=w_br_na.astype(BF16), w_out=w_out.astype(BF16),
        g_mix_pre=row(g_mix_pre), g_mix_post=row(g_mix_post), g_ca_pre=row(g_ca_pre), g_mem=row(g_mem),
        w_ca_q=w_ca_q.astype(BF16), w_ca_kv=w_ca_kv.astype(BF16), w_ca_o=w_ca_o.astype(BF16),
        g_ca_post=row(g_ca_post), g_ffn_pre=row(g_ffn_pre), w_ffn_gu=w_ffn_gu.astype(BF16),
        w_ffn_down=w_ffn_down.astype(BF16), g_ffn_post=row(g_ffn_post))


def _layer(x, mem, p):
    b, l, d = x.shape
    m = b * l
    na_w = NA_HEADS * NA_HEAD_DIM
    z_block, q_col, k_col, v_col = 0, d, d + na_w, d + 2 * na_w
    gs_col = d + 3 * na_w
    gn_col = gs_col + d
    xbc_col = gn_col + d
    assert xbc_col % CONV_CH == 0 and gs_col % d == 0 and gn_col % d == 0

    x2d = x.reshape(m, d)
    proj = _norm_matmul(x2d, p["g_mix_pre"], p["w_main"], BF16, _pick(m, 1024), 1024)
    dt, cs = _dt_scan(x2d, p["g_mix_pre"], p["w_dt"], p["a_log"], p["dt_bias"], _pick(l, 512))

    proj3 = proj.reshape(b, l, -1)
    xc = _conv_silu(proj3, p["conv_w"], p["conv_b"], xbc_col // CONV_CH, _pick(l, 512))
    y2 = _ssd(xc, dt.reshape(b, l, -1), cs.reshape(b, l, -1), p["d_skip"])

    y_na = _na(proj3, p["strips"], q_col // NA_HEAD_DIM, k_col // NA_HEAD_DIM, v_col // NA_HEAD_DIM)

    merged = _merge(y2.reshape(2, m, SSD_INNER), proj, y_na.reshape(m, na_w), p["g_ssd_out"],
                    p["b_gate"], p["w_br_ssd"], p["w_br_na"],
                    z_block, gs_col // d, gn_col // d, _pick(m, 256))
    x1 = _proj_norm_res(merged, p["w_out"], p["g_mix_post"], x2d, _pick(m, 512))

    n_mem = mem.shape[1]
    kv = _norm_matmul(mem.reshape(b * n_mem, d), p["g_mem"], p["w_ca_kv"], BF16, _pick(b * n_mem, 256), 1024)
    x2 = _xattn(x1.reshape(b, l, d), kv.reshape(b, n_mem, -1), p["g_ca_pre"], p["w_ca_q"],
                p["w_ca_o"], p["g_ca_post"], _pick(l, 512))

    x3 = _ffn(x2.reshape(m, d), p["g_ffn_pre"], p["w_ffn_gu"], p["w_ffn_down"], p["g_ffn_post"],
              _pick(m, 512), 512)
    return x3.reshape(b, l, d)


def kernel(x_prompt, x_sample, mem_prompt, mem_sample, w_in, b_gate, conv_w, conv_b, a_log, dt_bias, d_skip, g_ssd_out, rpb, w_br_ssd, w_br_na, w_out, g_mix_pre, g_mix_post, g_ca_pre, g_mem, w_ca_q, w_ca_kv, w_ca_o, g_ca_post, g_ffn_pre, w_ffn_gu, w_ffn_down, g_ffn_post):
    params = (w_in, b_gate, conv_w, conv_b, a_log, dt_bias, d_skip, g_ssd_out, rpb, w_br_ssd,
              w_br_na, w_out, g_mix_pre, g_mix_post, g_ca_pre, g_mem, w_ca_q, w_ca_kv, w_ca_o,
              g_ca_post, g_ffn_pre, w_ffn_gu, w_ffn_down, g_ffn_post)
    y_prompt, y_sample = x_prompt, x_sample
    for i in range(w_in.shape[0]):
        p = _prepare(*[w[i] for w in params])
        y_prompt = _layer(y_prompt, mem_prompt, p)
        y_sample = _layer(y_sample, mem_sample, p)
    return (y_prompt, y_sample)
```

```python
import functools
import math

import numpy as np
import jax
import jax.numpy as jnp
from jax import lax
from jax.experimental import pallas as pl
from jax.experimental.pallas import tpu as pltpu

BF16 = jnp.bfloat16
F32 = jnp.float32

EPS = 1e-6
LANES = 128
GRID_W = 64
SSD_HEADS = 32
SSD_HEAD_DIM = 64
SSD_GROUPS = 4
SSD_STATE = 128
SSD_INNER = SSD_HEADS * SSD_HEAD_DIM
SSD_CONV = 5
CONV_CH = SSD_INNER + 2 * SSD_GROUPS * SSD_STATE
CHUNK = 128
HALO = 16
NA_HEADS = 16
NA_HEAD_DIM = 128
NA_KR = 8
NA_KC = 16
CA_HEADS = 4
CA_HEAD_DIM = 128
NEG = -1e30
LOG2E = math.log2(math.e)

VMEM_LIMIT_BYTES = 56 * 1024 * 1024


def _cparams(*sem):
    return pltpu.CompilerParams(dimension_semantics=sem, vmem_limit_bytes=VMEM_LIMIT_BYTES)


def _resident(shape):
    return pl.BlockSpec(shape, lambda *_: (0,) * len(shape), pipeline_mode=pl.Buffered(1))


def _rms(x, g):
    return x * lax.rsqrt(jnp.mean(x * x, axis=-1, keepdims=True) + EPS) * g


def _dot(a, b):
    return jnp.dot(a, b, preferred_element_type=F32)


def _dot_nt(a, b):
    return lax.dot_general(a, b, (((1,), (1,)), ((), ())), preferred_element_type=F32)


def _norm_matmul_kernel(x_ref, g_ref, w_ref, o_ref, a_scr):
    @pl.when(pl.program_id(1) == 0)
    def _():
        a_scr[...] = _rms(x_ref[...], g_ref[...]).astype(BF16)

    o_ref[...] = _dot(a_scr[...], w_ref[...]).astype(o_ref.dtype)


def _norm_matmul(x, g, w, out_dtype, tm, tn):
    m, d = x.shape
    n = w.shape[1]
    return pl.pallas_call(
        _norm_matmul_kernel,
        grid=(m // tm, n // tn),
        in_specs=[
            pl.BlockSpec((tm, d), lambda i, j: (i, 0)),
            pl.BlockSpec((1, d), lambda i, j: (0, 0)),
            pl.BlockSpec((d, tn), lambda i, j: (0, j)),
        ],
        out_specs=pl.BlockSpec((tm, tn), lambda i, j: (i, j)),
        out_shape=jax.ShapeDtypeStruct((m, n), out_dtype),
        scratch_shapes=[pltpu.VMEM((tm, d), BF16)],
        compiler_params=_cparams("parallel", "arbitrary"),
        name="norm_matmul",
    )(x, g, w)


def _split3(x):
    hi = x.astype(BF16)
    r1 = x - hi.astype(F32)
    mid = r1.astype(BF16)
    lo = (r1 - mid.astype(F32)).astype(BF16)
    return hi, mid, lo


SCAN_ROWS = 512


def _dt_scan(a, w_ref, alog_ref, dtb_ref, dt_ref, cs_ref):
    ri = lax.broadcasted_iota(jnp.int32, (SCAN_ROWS, SCAN_ROWS), 0)
    ci = lax.broadcasted_iota(jnp.int32, (SCAN_ROWS, SCAN_ROWS), 1)
    same = (ri // CHUNK) == (ci // CHUNK)
    t_fwd = (same & (ri >= ci)).astype(BF16)
    t_bwd = (same & (ri <= ci)).astype(BF16)
    a_neg = -jnp.exp(alog_ref[...])
    for r0 in range(0, a.shape[0], SCAN_ROWS):
        rs = slice(r0, r0 + SCAN_ROWS)
        z = _dot(a[rs], w_ref[...]) + dtb_ref[...]
        dt = jnp.maximum(z, 0.0) + jnp.log1p(jnp.exp(-jnp.abs(z)))
        hi, mid, lo = _split3(dt * a_neg)
        dt_ref[rs, :] = dt
        cs_ref[rs, :LANES] = LOG2E * (
            _dot(t_fwd, hi[:, :LANES]) + _dot(t_fwd, mid[:, :LANES]) + _dot(t_fwd, lo[:, :LANES]))
        cs_ref[rs, LANES:] = LOG2E * (
            _dot(t_bwd, hi[:, LANES:]) + _dot(t_bwd, mid[:, LANES:]) + _dot(t_bwd, lo[:, LANES:]))


def _in_proj_kernel(x_ref, g_ref, w_ref, wdt_ref, alog_ref, dtb_ref, o_ref, dt_ref, cs_ref, a_scr):
    @pl.when(pl.program_id(1) == 0)
    def _():
        a = _rms(x_ref[...], g_ref[...]).astype(BF16)
        a_scr[...] = a
        _dt_scan(a, wdt_ref, alog_ref, dtb_ref, dt_ref, cs_ref)

    o_ref[...] = _dot(a_scr[...], w_ref[...]).astype(o_ref.dtype)


def _in_proj(x, g, w, w_dt, a_log, dt_bias, tm, tn):
    m, d = x.shape
    n = w.shape[1]
    nd = 2 * LANES
    small = jax.ShapeDtypeStruct((m, nd), F32)
    return pl.pallas_call(
        _in_proj_kernel,
        grid=(m // tm, n // tn),
        in_specs=[
            pl.BlockSpec((tm, d), lambda i, j: (i, 0)),
            _resident((1, d)),
            pl.BlockSpec((d, tn), lambda i, j: (0, j)),
            _resident((d, nd)),
            _resident((1, nd)),
            _resident((1, nd)),
        ],
        out_specs=[pl.BlockSpec((tm, tn), lambda i, j: (i, j)),
                   pl.BlockSpec((tm, nd), lambda i, j: (i, 0)),
                   pl.BlockSpec((tm, nd), lambda i, j: (i, 0))],
        out_shape=[jax.ShapeDtypeStruct((m, n), BF16), small, small],
        scratch_shapes=[pltpu.VMEM((tm, d), BF16)],
        compiler_params=_cparams("parallel", "arbitrary"),
        name="in_proj",
    )(x, g, w, w_dt, a_log, dt_bias)


def _conv_kernel(xm_ref, xp_ref, xn_ref, cw_ref, cb_ref, o_ref, xs_scr, *, n_tiles):
    i = pl.program_id(1)
    tc = xm_ref.shape[1]
    has_prev = (i > 0).astype(F32)
    has_next = (i < n_tiles - 1).astype(F32)
    xs_scr[0:HALO, :] = xp_ref[0].astype(F32) * has_prev
    xs_scr[HALO:HALO + tc, :] = xm_ref[0].astype(F32)
    xs_scr[HALO + tc:, :] = xn_ref[0].astype(F32) * has_next
    for r in range(tc // CHUNK):
        acc = jnp.broadcast_to(cb_ref[...], (CHUNK, CONV_CH))
        for k in range(SSD_CONV):
            off = r * CHUNK + HALO - SSD_CONV // 2 + k
            acc = acc + cw_ref[k:k + 1, :] * xs_scr[off:off + CHUNK, :]
        o_ref[0, r * CHUNK:(r + 1) * CHUNK, :] = (acc * jax.nn.sigmoid(acc)).astype(o_ref.dtype)


def _conv_silu(proj, conv_w, conv_b, xbc_block, tc):
    b, l, _ = proj.shape
    n_tiles = l // tc
    hpt = tc // HALO
    kern = functools.partial(_conv_kernel, n_tiles=n_tiles)
    return pl.pallas_call(
        kern,
        grid=(b, n_tiles),
        in_specs=[
            pl.BlockSpec((1, tc, CONV_CH), lambda bi, i: (bi, i, xbc_block)),
            pl.BlockSpec((1, HALO, CONV_CH), lambda bi, i: (bi, jnp.maximum(i * hpt - 1, 0), xbc_block)),
            pl.BlockSpec((1, HALO, CONV_CH),
                         lambda bi, i: (bi, jnp.minimum((i + 1) * hpt, n_tiles * hpt - 1), xbc_block)),
            pl.BlockSpec((8, CONV_CH), lambda bi, i: (0, 0)),
            pl.BlockSpec((1, CONV_CH), lambda bi, i: (0, 0)),
        ],
        out_specs=pl.BlockSpec((1, tc, CONV_CH), lambda bi, i: (bi, i, 0)),
        out_shape=jax.ShapeDtypeStruct((b, l, CONV_CH), BF16),
        scratch_shapes=[pltpu.VMEM((tc + 2 * HALO, CONV_CH), F32)],
        compiler_params=_cparams("parallel", "parallel"),
        name="conv_silu",
    )(proj, proj, proj, conv_w, conv_b)


def _ssd_chunk(d, r0, xc_ref, dt_ref, cs_ref, dskip, y_ref, st_scr):
    rows = pl.ds(r0, CHUNK)
    cs = cs_ref[0, rows, :]
    cs_t = cs.T
    dt_t = dt_ref[0, rows, :].T
    tot_col = jnp.where(d == 0, cs_t[:, CHUNK - 1:CHUNK], cs_t[:, 0:1])
    tot_b = jnp.broadcast_to(tot_col, (CHUNK, CHUNK))
    wst_t = jnp.exp2(tot_b - cs_t) * dt_t
    cdec = jnp.exp2(tot_b)
    csd_t = cs_t - jnp.log2(dt_t)

    ri = lax.broadcasted_iota(jnp.int32, (CHUNK, CHUNK), 0)
    ci = lax.broadcasted_iota(jnp.int32, (CHUNK, CHUNK), 1)
    causal = jnp.where(d == 0, ri - ci, ci - ri) >= 0
    first = lax.broadcasted_iota(jnp.int32, (1, 2 * SSD_HEAD_DIM), 1) < SSD_HEAD_DIM

    gn = SSD_GROUPS * SSD_STATE
    pairs = SSD_HEADS // SSD_GROUPS // 2
    for g in range(SSD_GROUPS):
        b_g = xc_ref[0, rows, SSD_INNER + g * SSD_STATE:SSD_INNER + (g + 1) * SSD_STATE]
        c_g = xc_ref[0, rows, SSD_INNER + gn + g * SSD_STATE:SSD_INNER + gn + (g + 1) * SSD_STATE]
        gmat = _dot_nt(c_g, b_g)
        c_f = c_g.astype(F32)
        b_t = b_g.astype(F32).T
        for j in range(pairs):
            h0 = (g * pairs + j) * 2
            ps = slice(h0 * SSD_HEAD_DIM, (h0 + 2) * SSD_HEAD_DIM)
            x2 = xc_ref[0, rows, ps]
            s2 = st_scr[:, ps]
            s2_bf = s2.astype(BF16)
            zero = jnp.zeros_like(x2)
            lhs, bw, rhs, xs = [], [], [], []
            for e in range(2):
                h = h0 + e
                col = jnp.broadcast_to(cs[:, h:h + 1], (CHUNK, CHUNK))
                lmat_dt = jnp.exp2(jnp.where(causal, col - csd_t[h:h + 1, :], NEG))
                lhs.append((gmat * lmat_dt).astype(BF16))
                lhs.append((c_f * jnp.exp2(col)).astype(BF16))
                bw.append((b_t * wst_t[h:h + 1, :]).astype(BF16))
                keep = first if e == 0 else jnp.logical_not(first)
                xe = jnp.where(keep, x2, zero)
                xs.append(xe)
                rhs.append(xe)
                rhs.append(jnp.where(keep, s2_bf, zero))
            y = _dot(jnp.concatenate(lhs, axis=1), jnp.concatenate(rhs, axis=0))
            y = y + dskip[:, ps] * x2.astype(F32)
            y_ref[0, 0, rows, ps] = y.astype(y_ref.dtype)
            s_new = _dot(jnp.concatenate(bw, axis=1), jnp.concatenate(xs, axis=0))
            cd = jnp.where(first, cdec[h0:h0 + 1, :], cdec[h0 + 1:h0 + 2, :])
            st_scr[:, ps] = s2 * cd + s_new


def _ssd_kernel(xc_ref, dt_ref, cs_ref, dsk_ref, y_ref, st_scr):
    d = pl.program_id(1)
    n_sub = xc_ref.shape[1] // CHUNK

    @pl.when(pl.program_id(2) == 0)
    def _():
        st_scr[...] = jnp.zeros_like(st_scr)

    dskip = jnp.where(d == 0, 1.0, 0.0) * dsk_ref[...]

    def body(k, carry):
        sub = jnp.where(d == 0, k, n_sub - 1 - k)
        _ssd_chunk(d, pl.multiple_of(sub * CHUNK, CHUNK), xc_ref, dt_ref, cs_ref, dskip, y_ref, st_scr)
        return carry

    lax.fori_loop(0, n_sub, body, 0)


def _ssd(xc, dt, cs, d_skip, rows_per_step):
    b, l, _ = xc.shape
    tr = rows_per_step
    nb = l // tr

    def blk(d, c):
        return jnp.where(d == 0, c, nb - 1 - c)

    return pl.pallas_call(
        _ssd_kernel,
        grid=(b, 2, nb),
        in_specs=[
            pl.BlockSpec((1, tr, CONV_CH), lambda i, d, c: (i, blk(d, c), 0)),
            pl.BlockSpec((1, tr, LANES), lambda i, d, c: (i, blk(d, c), d)),
            pl.BlockSpec((1, tr, LANES), lambda i, d, c: (i, blk(d, c), d)),
            _resident((1, SSD_INNER)),
        ],
        out_specs=pl.BlockSpec((1, 1, tr, SSD_INNER), lambda i, d, c: (d, i, blk(d, c), 0)),
        out_shape=jax.ShapeDtypeStruct((2, b, l, SSD_INNER), BF16),
        scratch_shapes=[pltpu.VMEM((SSD_STATE, SSD_INNER), F32)],
        compiler_params=_cparams("parallel", "arbitrary", "arbitrary"),
        name="ssd_scan",
    )(xc, dt, cs, d_skip)


NA_UNROLL = 32
NA_HEADS_PER_STEP = 2


def _na_kernel(q_ref, k_ref, v_ref, strip_ref, o_ref, *, rows):
    win = NA_KR * GRID_W
    c1 = (NA_HEAD_DIM ** -0.5) * LOG2E

    def body(hh, it, carry):
        hs = slice(hh * NA_HEAD_DIM, (hh + 1) * NA_HEAD_DIM)
        q0s, k0s, scores = [], [], []
        for u in range(NA_UNROLL):
            r = it * NA_UNROLL + u
            rs = jnp.clip(r - NA_KR // 2, 0, rows - NA_KR)
            q0 = pl.multiple_of(r * GRID_W, GRID_W)
            k0 = pl.multiple_of(rs * GRID_W, GRID_W)
            q0s.append(q0)
            k0s.append(k0)
            scores.append(_dot_nt(q_ref[0, pl.ds(q0, GRID_W), hs], k_ref[0, pl.ds(k0, win), hs]) * c1
                          + strip_ref[hh, rs - r + NA_KR - 1])
        probs, dens = [], []
        for s in scores:
            m = jnp.max(s, axis=-1, keepdims=True)
            p = jnp.exp2(s - m)
            dens.append(jnp.sum(p, axis=-1, keepdims=True))
            probs.append(p.astype(BF16))
        for u in range(NA_UNROLL):
            o = _dot(probs[u], v_ref[0, pl.ds(k0s[u], win), hs]) / dens[u]
            o_ref[0, pl.ds(q0s[u], GRID_W), hs] = o.astype(o_ref.dtype)
        return carry

    for hh in range(NA_HEADS_PER_STEP):
        lax.fori_loop(0, rows // NA_UNROLL, functools.partial(body, hh), 0)


def _na(proj, strips, q_block, k_block, v_block):
    b, l, _ = proj.shape
    rows = l // GRID_W
    win = NA_KR * GRID_W
    hps = NA_HEADS_PER_STEP
    wide = hps * NA_HEAD_DIM
    assert q_block % hps == 0 and k_block % hps == 0 and v_block % hps == 0
    assert rows % NA_UNROLL == 0
    kern = functools.partial(_na_kernel, rows=rows)
    return pl.pallas_call(
        kern,
        grid=(b, NA_HEADS // hps),
        in_specs=[
            pl.BlockSpec((1, l, wide), lambda bi, h: (bi, 0, q_block // hps + h)),
            pl.BlockSpec((1, l, wide), lambda bi, h: (bi, 0, k_block // hps + h)),
            pl.BlockSpec((1, l, wide), lambda bi, h: (bi, 0, v_block // hps + h)),
            pl.BlockSpec((hps, NA_KR, GRID_W, win), lambda bi, h: (h, 0, 0, 0)),
        ],
        out_specs=pl.BlockSpec((1, l, wide), lambda bi, h: (bi, 0, h)),
        out_shape=jax.ShapeDtypeStruct((b, l, NA_HEADS * NA_HEAD_DIM), BF16),
        compiler_params=_cparams("parallel", "parallel"),
        name="neighbourhood_attn",
    )(proj, proj, proj, strips)


def _na_strips(rpb):
    qc = np.arange(GRID_W)[:, None]
    kc = np.arange(GRID_W)[None, :]
    cstart = np.clip(qc - NA_KC // 2, 0, GRID_W - NA_KC)
    valid = (kc >= cstart) & (kc < cstart + NA_KC)
    dcidx = np.clip(kc - qc + NA_KC - 1, 0, 2 * NA_KC - 2)
    tb = jnp.where(valid[None, None], rpb[:, :, dcidx].astype(F32) * LOG2E, NEG)
    return jnp.stack(
        [jnp.concatenate([tb[:, o + j] for j in range(NA_KR)], axis=-1) for o in range(NA_KR)], axis=1)


def _merge_kernel(yf_ref, yb_ref, z_ref, yna_ref, gs_ref, gn_ref, gso_ref, bg_ref, w1_ref, w2_ref, o_ref):
    g_na = jax.nn.sigmoid(gn_ref[...].astype(F32) + bg_ref[1:2, :])
    br_na = g_na * _dot(yna_ref[...], w2_ref[...])
    y = yf_ref[0].astype(F32) + yb_ref[0].astype(F32)
    zf = z_ref[...].astype(F32)
    u = _rms(y * (zf * jax.nn.sigmoid(zf)), gso_ref[...]).astype(BF16)
    g_ssd = jax.nn.sigmoid(gs_ref[...].astype(F32) + bg_ref[0:1, :])
    o_ref[...] = (g_ssd * _dot(u, w1_ref[...]) + br_na).astype(o_ref.dtype)


def _merge(y2, proj, yna, g_ssd_out, b_gate, w1, w2, z_block, gs_block, gn_block, tm):
    _, m, d = y2.shape
    return pl.pallas_call(
        _merge_kernel,
        grid=(m // tm,),
        in_specs=[
            pl.BlockSpec((1, tm, d), lambda i: (0, i, 0)),
            pl.BlockSpec((1, tm, d), lambda i: (1, i, 0)),
            pl.BlockSpec((tm, d), lambda i: (i, z_block)),
            pl.BlockSpec((tm, d), lambda i: (i, 0)),
            pl.BlockSpec((tm, d), lambda i: (i, gs_block)),
            pl.BlockSpec((tm, d), lambda i: (i, gn_block)),
            _resident((1, d)),
            _resident((2, d)),
            _resident((d, d)),
            _resident((d, d)),
        ],
        out_specs=pl.BlockSpec((tm, d), lambda i: (i, 0)),
        out_shape=jax.ShapeDtypeStruct((m, d), BF16),
        compiler_params=_cparams("parallel"),
        name="branch_merge",
    )(y2, y2, proj, yna, proj, proj, g_ssd_out, b_gate, w1, w2)


def _out_xattn_kernel(a_ref, x_ref, kv_ref, wout_ref, gmix_ref, gpre_ref, wq_ref, wo_ref, gpost_ref, o_ref,
                      *, parts):
    tm = x_ref.shape[1]
    rows = [slice(i * tm // parts, (i + 1) * tm // parts) for i in range(parts)]
    width = CA_HEADS * CA_HEAD_DIM
    heads = [slice(h * CA_HEAD_DIM, (h + 1) * CA_HEAD_DIM) for h in range(CA_HEADS)]
    proj = [_dot(a_ref[0, r, :], wout_ref[...]) for r in rows]
    x1 = [x_ref[0, r, :] + _rms(pj, gmix_ref[...]) for r, pj in zip(rows, proj)]
    q = [_dot(_rms(xi, gpre_ref[...]).astype(BF16), wq_ref[...]).astype(BF16) for xi in x1]
    scores = [[_dot_nt(qi[:, hs], kv_ref[0, :, hs]) * (CA_HEAD_DIM ** -0.5) for hs in heads] for qi in q]
    probs, dens = [], []
    for sc in scores:
        pp, dd = [], []
        for s in sc:
            m = jnp.max(s, axis=-1, keepdims=True)
            p = jnp.exp(s - m)
            dd.append(jnp.sum(p, axis=-1, keepdims=True))
            pp.append(p.astype(BF16))
        probs.append(pp)
        dens.append(dd)
    att = [jnp.concatenate(
        [(_dot(pp[h], kv_ref[0, :, width + h * CA_HEAD_DIM:width + (h + 1) * CA_HEAD_DIM]) / dd[h]).astype(BF16)
         for h in range(CA_HEADS)], axis=1) for pp, dd in zip(probs, dens)]
    cproj = [_dot(o, wo_ref[...]) for o in att]
    for r, xi, cp in zip(rows, x1, cproj):
        o_ref[0, r, :] = xi + _rms(cp, gpost_ref[...])


def _out_xattn(a, x, kv, w_out, g_mix_post, g_pre, wq, wo, g_post, tm, parts):
    b, l, d = x.shape
    n_mem, kvw = kv.shape[1], kv.shape[2]
    width = wq.shape[1]
    return pl.pallas_call(
        functools.partial(_out_xattn_kernel, parts=parts),
        grid=(b, l // tm),
        in_specs=[
            pl.BlockSpec((1, tm, d), lambda bi, i: (bi, i, 0)),
            pl.BlockSpec((1, tm, d), lambda bi, i: (bi, i, 0)),
            pl.BlockSpec((1, n_mem, kvw), lambda bi, i: (bi, 0, 0)),
            _resident((d, d)),
            _resident((1, d)),
            _resident((1, d)),
            _resident((d, width)),
            _resident((width, d)),
            _resident((1, d)),
        ],
        out_specs=pl.BlockSpec((1, tm, d), lambda bi, i: (bi, i, 0)),
        out_shape=jax.ShapeDtypeStruct((b, l, d), F32),
        compiler_params=_cparams("parallel", "parallel"),
        name="out_proj_xattn",
    )(a, x, kv, w_out, g_mix_post, g_pre, wq, wo, g_post)


def _ffn_kernel(x_ref, gpre_ref, wg_ref, wu_ref, wd_ref, gpost_ref, o_ref, a_scr, acc_scr):
    f = pl.program_id(1)

    @pl.when(f == 0)
    def _():
        a_scr[...] = _rms(x_ref[...], gpre_ref[...]).astype(BF16)
        acc_scr[...] = jnp.zeros_like(acc_scr)

    a = a_scr[...]
    hg = _dot(a, wg_ref[...])
    hu = _dot(a, wu_ref[...])
    act = (hg * jax.nn.sigmoid(hg) * hu).astype(BF16)
    acc_scr[...] += _dot(act, wd_ref[...])

    @pl.when(f == pl.num_programs(1) - 1)
    def _():
        o_ref[...] = x_ref[...] + _rms(acc_scr[...], gpost_ref[...])


def _ffn(x, g_pre, w_gu, w_down, g_post, tm, tf):
    m, d = x.shape
    dff = w_down.shape[0]
    nf = dff // tf
    return pl.pallas_call(
        _ffn_kernel,
        grid=(m // tm, nf),
        in_specs=[
            pl.BlockSpec((tm, d), lambda i, f: (i, 0)),
            pl.BlockSpec((1, d), lambda i, f: (0, 0)),
            pl.BlockSpec((d, tf), lambda i, f: (0, f)),
            pl.BlockSpec((d, tf), lambda i, f: (0, nf + f)),
            pl.BlockSpec((tf, d), lambda i, f: (f, 0)),
            pl.BlockSpec((1, d), lambda i, f: (0, 0)),
        ],
        out_specs=pl.BlockSpec((tm, d), lambda i, f: (i, 0)),
        out_shape=jax.ShapeDtypeStruct((m, d), F32),
        scratch_shapes=[pltpu.VMEM((tm, d), BF16), pltpu.VMEM((tm, d), F32)],
        compiler_params=_cparams("parallel", "arbitrary"),
        name="swiglu_ffn",
    )(x, g_pre, w_gu, w_gu, w_down, g_post)


def _pick(n, target):
    t = min(n, target)
    while n % t:
        t //= 2
    return t


def _prepare(w_in, b_gate, conv_w, conv_b, a_log, dt_bias, d_skip, g_ssd_out, rpb,
             w_br_ssd, w_br_na, w_out, g_mix_pre, g_mix_post, g_ca_pre, g_mem, w_ca_q, w_ca_kv,
             w_ca_o, g_ca_post, g_ffn_pre, w_ffn_gu, w_ffn_down, g_ffn_post):
    d = w_in.shape[0]
    row = lambda v: v.reshape(1, -1).astype(F32)
    o1 = SSD_INNER
    o2 = o1 + CONV_CH
    o3 = o2 + 2 * SSD_HEADS
    lane_pad = jnp.zeros((d, LANES - SSD_HEADS), w_in.dtype)
    head_pad = jnp.zeros((2, LANES - SSD_HEADS), F32)
    lanes2 = lambda v: jnp.concatenate([v.astype(F32).reshape(2, SSD_HEADS), head_pad], axis=1).reshape(1, -1)
    return dict(
        w_main=jnp.concatenate([w_in[:, :o1].astype(BF16), w_in[:, o3:].astype(BF16),
                                w_in[:, o1:o2].astype(BF16)], axis=1),
        w_dt=jnp.concatenate([w_in[:, o2:o2 + SSD_HEADS], lane_pad, w_in[:, o2 + SSD_HEADS:o3], lane_pad],
                             axis=1).astype(BF16),
        a_log=lanes2(a_log), dt_bias=lanes2(dt_bias),
        conv_w=jnp.pad(conv_w.astype(F32), ((0, 8 - SSD_CONV), (0, 0))), conv_b=row(conv_b),
        d_skip=row(jnp.repeat(d_skip, SSD_HEAD_DIM)), g_ssd_out=row(g_ssd_out),
        strips=_na_strips(rpb), b_gate=b_gate.astype(F32),
        w_br_ssd=w_br_ssd.astype(BF16), w_br_na=w_br_na.astype(BF16), w_out=w_out.astype(BF16),
        g_mix_pre=row(g_mix_pre), g_mix_post=row(g_mix_post), g_ca_pre=row(g_ca_pre), g_mem=row(g_mem),
        w_ca_q=w_ca_q.astype(BF16), w_ca_kv=w_ca_kv.astype(BF16), w_ca_o=w_ca_o.astype(BF16),
        g_ca_post=row(g_ca_post), g_ffn_pre=row(g_ffn_pre), w_ffn_gu=w_ffn_gu.astype(BF16),
        w_ffn_down=w_ffn_down.astype(BF16), g_ffn_post=row(g_ffn_post))


def _layer(x, mem, p):
    b, l, d = x.shape
    m = b * l
    na_w = NA_HEADS * NA_HEAD_DIM
    z_block, q_col, k_col, v_col = 0, d, d + na_w, d + 2 * na_w
    gs_col = d + 3 * na_w
    gn_col = gs_col + d
    xbc_col = gn_col + d
    assert xbc_col % CONV_CH == 0 and gs_col % d == 0 and gn_col % d == 0

    x2d = x.reshape(m, d)
    proj, dt, cs = _in_proj(x2d, p["g_mix_pre"], p["w_main"], p["w_dt"], p["a_log"], p["dt_bias"],
                            _pick(l, 1024), 1536)

    proj3 = proj.reshape(b, l, -1)
    xc = _conv_silu(proj3, p["conv_w"], p["conv_b"], xbc_col // CONV_CH, _pick(l, 512))
    y2 = _ssd(xc, dt.reshape(b, l, -1), cs.reshape(b, l, -1), p["d_skip"], _pick(l, 1024))

    y_na = _na(proj3, p["strips"], q_col // NA_HEAD_DIM, k_col // NA_HEAD_DIM, v_col // NA_HEAD_DIM)

    merged = _merge(y2.reshape(2, m, SSD_INNER), proj, y_na.reshape(m, na_w), p["g_ssd_out"],
                    p["b_gate"], p["w_br_ssd"], p["w_br_na"],
                    z_block, gs_col // d, gn_col // d, _pick(m, 256))
    n_mem = mem.shape[1]
    kv = _norm_matmul(mem.reshape(b * n_mem, d), p["g_mem"], p["w_ca_kv"], BF16, _pick(b * n_mem, 256), 1024)
    x2 = _out_xattn(merged.reshape(b, l, d), x, kv.reshape(b, n_mem, -1), p["w_out"], p["g_mix_post"],
                    p["g_ca_pre"], p["w_ca_q"], p["w_ca_o"], p["g_ca_post"], _pick(l, 512), 2)

    x3 = _ffn(x2.reshape(m, d), p["g_ffn_pre"], p["w_ffn_gu"], p["w_ffn_down"], p["g_ffn_post"],
              _pick(m, 512), 512)
    return x3.reshape(b, l, d)


def kernel(x_prompt, x_sample, mem_prompt, mem_sample, w_in, b_gate, conv_w, conv_b, a_log, dt_bias, d_skip, g_ssd_out, rpb, w_br_ssd, w_br_na, w_out, g_mix_pre, g_mix_post, g_ca_pre, g_mem, w_ca_q, w_ca_kv, w_ca_o, g_ca_post, g_ffn_pre, w_ffn_gu, w_ffn_down, g_ffn_post):
    params = (w_in, b_gate, conv_w, conv_b, a_log, dt_bias, d_skip, g_ssd_out, rpb, w_br_ssd,
              w_br_na, w_out, g_mix_pre, g_mix_post, g_ca_pre, g_mem, w_ca_q, w_ca_kv, w_ca_o,
              g_ca_post, g_ffn_pre, w_ffn_gu, w_ffn_down, g_ffn_post)
    y_prompt, y_sample = x_prompt, x_sample
    for i in range(w_in.shape[0]):
        p = _prepare(*[w[i] for w in params])
        y_prompt = _layer(y_prompt, mem_prompt, p)
        y_sample = _layer(y_sample, mem_sample, p)
    return (y_prompt, y_sample)
```

```python
import functools
import math

import numpy as np
import jax
import jax.numpy as jnp
from jax import lax
from jax.experimental import pallas as pl
from jax.experimental.pallas import tpu as pltpu

BF16 = jnp.bfloat16
F32 = jnp.float32

EPS = 1e-6
LANES = 128
GRID_W = 64
SSD_HEADS = 32
SSD_HEAD_DIM = 64
SSD_GROUPS = 4
SSD_STATE = 128
SSD_INNER = SSD_HEADS * SSD_HEAD_DIM
SSD_CONV = 5
CONV_CH = SSD_INNER + 2 * SSD_GROUPS * SSD_STATE
CHUNK = 128
HALO = 16
NA_HEADS = 16
NA_HEAD_DIM = 128
NA_KR = 8
NA_KC = 16
CA_HEADS = 4
CA_HEAD_DIM = 128
NEG = -1e30
LOG2E = math.log2(math.e)

VMEM_LIMIT_BYTES = 56 * 1024 * 1024


def _cparams(*sem):
    return pltpu.CompilerParams(dimension_semantics=sem, vmem_limit_bytes=VMEM_LIMIT_BYTES)


def _resident(shape):
    return pl.BlockSpec(shape, lambda *_: (0,) * len(shape), pipeline_mode=pl.Buffered(1))


def _rms(x, g):
    return x * lax.rsqrt(jnp.mean(x * x, axis=-1, keepdims=True) + EPS) * g


def _dot(a, b):
    return jnp.dot(a, b, preferred_element_type=F32)


def _dot_nt(a, b):
    return lax.dot_general(a, b, (((1,), (1,)), ((), ())), preferred_element_type=F32)


def _norm_matmul_kernel(x_ref, g_ref, w_ref, o_ref, a_scr):
    @pl.when(pl.program_id(1) == 0)
    def _():
        a_scr[...] = _rms(x_ref[...], g_ref[...]).astype(BF16)

    o_ref[...] = _dot(a_scr[...], w_ref[...]).astype(o_ref.dtype)


def _norm_matmul(x, g, w, out_dtype, tm, tn):
    m, d = x.shape
    n = w.shape[1]
    return pl.pallas_call(
        _norm_matmul_kernel,
        grid=(m // tm, n // tn),
        in_specs=[
            pl.BlockSpec((tm, d), lambda i, j: (i, 0)),
            pl.BlockSpec((1, d), lambda i, j: (0, 0)),
            pl.BlockSpec((d, tn), lambda i, j: (0, j)),
        ],
        out_specs=pl.BlockSpec((tm, tn), lambda i, j: (i, j)),
        out_shape=jax.ShapeDtypeStruct((m, n), out_dtype),
        scratch_shapes=[pltpu.VMEM((tm, d), BF16)],
        compiler_params=_cparams("parallel", "arbitrary"),
        name="norm_matmul",
    )(x, g, w)


def _split3(x):
    hi = x.astype(BF16)
    r1 = x - hi.astype(F32)
    mid = r1.astype(BF16)
    lo = (r1 - mid.astype(F32)).astype(BF16)
    return hi, mid, lo


SCAN_ROWS = 512


def _dt_scan(raw, alog_ref, dtb_ref, dt_ref, cs_ref):
    ri = lax.broadcasted_iota(jnp.int32, (SCAN_ROWS, SCAN_ROWS), 0)
    ci = lax.broadcasted_iota(jnp.int32, (SCAN_ROWS, SCAN_ROWS), 1)
    same = (ri // CHUNK) == (ci // CHUNK)
    t_fwd = (same & (ri >= ci)).astype(BF16)
    t_bwd = (same & (ri <= ci)).astype(BF16)
    a_neg = -jnp.exp(alog_ref[...])
    for r0 in range(0, raw.shape[0], SCAN_ROWS):
        rs = slice(r0, r0 + SCAN_ROWS)
        z = raw[rs] + dtb_ref[...]
        dt = jnp.maximum(z, 0.0) + jnp.log1p(jnp.exp(-jnp.abs(z)))
        hi, mid, lo = _split3(dt * a_neg)
        dt_ref[rs, :] = dt
        cs_ref[rs, :LANES] = LOG2E * (
            _dot(t_fwd, hi[:, :LANES]) + _dot(t_fwd, mid[:, :LANES]) + _dot(t_fwd, lo[:, :LANES]))
        cs_ref[rs, LANES:] = LOG2E * (
            _dot(t_bwd, hi[:, LANES:]) + _dot(t_bwd, mid[:, LANES:]) + _dot(t_bwd, lo[:, LANES:]))


def _in_proj_kernel(x_ref, g_ref, w_ref, wdt_ref, alog_ref, dtb_ref, o_ref, dt_ref, cs_ref, a_scr):
    @pl.when(pl.program_id(1) == 0)
    def _():
        a = _rms(x_ref[...], g_ref[...]).astype(BF16)
        a_scr[...] = a
        raw = _dot(a, wdt_ref[...])
        o_ref[...] = _dot(a, w_ref[...]).astype(o_ref.dtype)
        _dt_scan(raw, alog_ref, dtb_ref, dt_ref, cs_ref)

    @pl.when(pl.program_id(1) != 0)
    def _():
        o_ref[...] = _dot(a_scr[...], w_ref[...]).astype(o_ref.dtype)


def _in_proj(x, g, w, w_dt, a_log, dt_bias, tm, tn):
    m, d = x.shape
    n = w.shape[1]
    nd = 2 * LANES
    small = jax.ShapeDtypeStruct((m, nd), F32)
    return pl.pallas_call(
        _in_proj_kernel,
        grid=(m // tm, n // tn),
        in_specs=[
            pl.BlockSpec((tm, d), lambda i, j: (i, 0)),
            _resident((1, d)),
            pl.BlockSpec((d, tn), lambda i, j: (0, j)),
            _resident((d, nd)),
            _resident((1, nd)),
            _resident((1, nd)),
        ],
        out_specs=[pl.BlockSpec((tm, tn), lambda i, j: (i, j)),
                   pl.BlockSpec((tm, nd), lambda i, j: (i, 0)),
                   pl.BlockSpec((tm, nd), lambda i, j: (i, 0))],
        out_shape=[jax.ShapeDtypeStruct((m, n), BF16), small, small],
        scratch_shapes=[pltpu.VMEM((tm, d), BF16)],
        compiler_params=_cparams("parallel", "arbitrary"),
        name="in_proj",
    )(x, g, w, w_dt, a_log, dt_bias)


def _conv_kernel(xm_ref, xp_ref, xn_ref, cw_ref, cb_ref, o_ref, xs_scr, *, n_tiles):
    i = pl.program_id(1)
    tc = xm_ref.shape[1]
    xs_scr[0:HALO, :] = xp_ref[0] * (i > 0).astype(BF16)
    xs_scr[HALO:HALO + tc, :] = xm_ref[0]
    xs_scr[HALO + tc:, :] = xn_ref[0] * (i < n_tiles - 1).astype(BF16)
    win = CHUNK + 2 * HALO
    taps = [k for k in range(SSD_CONV) if k != SSD_CONV // 2]
    ri = lax.broadcasted_iota(jnp.int32, (len(taps) * CHUNK, win), 0)
    ci = lax.broadcasted_iota(jnp.int32, (len(taps) * CHUNK, win), 1)
    shift = jnp.zeros_like(ri)
    for n, k in enumerate(taps):
        shift = jnp.where(ri // CHUNK == n, k - SSD_CONV // 2, shift)
    select = (ci == ri % CHUNK + HALO + shift).astype(BF16)
    for r in range(tc // CHUNK):
        window = xs_scr[r * CHUNK:r * CHUNK + win, :]
        shifted = _dot(select, window)
        acc = cb_ref[...] + cw_ref[SSD_CONV // 2:SSD_CONV // 2 + 1, :] * window[HALO:HALO + CHUNK].astype(F32)
        for n, k in enumerate(taps):
            acc = acc + cw_ref[k:k + 1, :] * shifted[n * CHUNK:(n + 1) * CHUNK]
        o_ref[0, r * CHUNK:(r + 1) * CHUNK, :] = (acc * jax.nn.sigmoid(acc)).astype(o_ref.dtype)


def _conv_silu(proj, conv_w, conv_b, xbc_block, tc):
    b, l, _ = proj.shape
    n_tiles = l // tc
    hpt = tc // HALO
    kern = functools.partial(_conv_kernel, n_tiles=n_tiles)
    return pl.pallas_call(
        kern,
        grid=(b, n_tiles),
        in_specs=[
            pl.BlockSpec((1, tc, CONV_CH), lambda bi, i: (bi, i, xbc_block)),
            pl.BlockSpec((1, HALO, CONV_CH), lambda bi, i: (bi, jnp.maximum(i * hpt - 1, 0), xbc_block)),
            pl.BlockSpec((1, HALO, CONV_CH),
                         lambda bi, i: (bi, jnp.minimum((i + 1) * hpt, n_tiles * hpt - 1), xbc_block)),
            pl.BlockSpec((8, CONV_CH), lambda bi, i: (0, 0)),
            pl.BlockSpec((1, CONV_CH), lambda bi, i: (0, 0)),
        ],
        out_specs=pl.BlockSpec((1, tc, CONV_CH), lambda bi, i: (bi, i, 0)),
        out_shape=jax.ShapeDtypeStruct((b, l, CONV_CH), BF16),
        scratch_shapes=[pltpu.VMEM((tc + 2 * HALO, CONV_CH), BF16)],
        compiler_params=_cparams("parallel", "parallel"),
        name="conv_silu",
    )(proj, proj, proj, conv_w, conv_b)


def _ssd_chunk(d, r0, xc_ref, dt_ref, cs_ref, dskip, y_ref, st_scr):
    rows = pl.ds(r0, CHUNK)
    cs = cs_ref[0, rows, :]
    cs_t = cs.T
    dt_t = dt_ref[0, rows, :].T
    tot_col = jnp.where(d == 0, cs_t[:, CHUNK - 1:CHUNK], cs_t[:, 0:1])
    tot_b = jnp.broadcast_to(tot_col, (CHUNK, CHUNK))
    wst_t = jnp.exp2(tot_b - cs_t) * dt_t
    cdec = jnp.exp2(tot_b)
    csd_t = cs_t - jnp.log2(dt_t)

    ri = lax.broadcasted_iota(jnp.int32, (CHUNK, CHUNK), 0)
    ci = lax.broadcasted_iota(jnp.int32, (CHUNK, CHUNK), 1)
    causal = jnp.where(d == 0, ri - ci, ci - ri) >= 0
    first = lax.broadcasted_iota(jnp.int32, (1, 2 * SSD_HEAD_DIM), 1) < SSD_HEAD_DIM

    gn = SSD_GROUPS * SSD_STATE
    pairs = SSD_HEADS // SSD_GROUPS // 2
    for g in range(SSD_GROUPS):
        b_g = xc_ref[0, rows, SSD_INNER + g * SSD_STATE:SSD_INNER + (g + 1) * SSD_STATE]
        c_g = xc_ref[0, rows, SSD_INNER + gn + g * SSD_STATE:SSD_INNER + gn + (g + 1) * SSD_STATE]
        gmat = _dot_nt(c_g, b_g)
        c_f = c_g.astype(F32)
        b_t = b_g.astype(F32).T
        for j in range(pairs):
            h0 = (g * pairs + j) * 2
            ps = slice(h0 * SSD_HEAD_DIM, (h0 + 2) * SSD_HEAD_DIM)
            x2 = xc_ref[0, rows, ps]
            s2 = st_scr[:, ps]
            s2_bf = s2.astype(BF16)
            zero = jnp.zeros_like(x2)
            lhs, bw, rhs, xs = [], [], [], []
            for e in range(2):
                h = h0 + e
                col = jnp.broadcast_to(cs[:, h:h + 1], (CHUNK, CHUNK))
                lmat_dt = jnp.exp2(jnp.where(causal, col - csd_t[h:h + 1, :], NEG))
                lhs.append((gmat * lmat_dt).astype(BF16))
                lhs.append((c_f * jnp.exp2(col)).astype(BF16))
                bw.append((b_t * wst_t[h:h + 1, :]).astype(BF16))
                keep = first if e == 0 else jnp.logical_not(first)
                xe = jnp.where(keep, x2, zero)
                xs.append(xe)
                rhs.append(xe)
                rhs.append(jnp.where(keep, s2_bf, zero))
            y = _dot(jnp.concatenate(lhs, axis=1), jnp.concatenate(rhs, axis=0))
            y = y + dskip[:, ps] * x2.astype(F32)
            y_ref[0, 0, rows, ps] = y.astype(y_ref.dtype)
            s_new = _dot(jnp.concatenate(bw, axis=1), jnp.concatenate(xs, axis=0))
            cd = jnp.where(first, cdec[h0:h0 + 1, :], cdec[h0 + 1:h0 + 2, :])
            st_scr[:, ps] = s2 * cd + s_new


def _ssd_kernel(xc_ref, dt_ref, cs_ref, dsk_ref, y_ref, st_scr):
    d = pl.program_id(1)
    n_sub = xc_ref.shape[1] // CHUNK

    @pl.when(pl.program_id(2) == 0)
    def _():
        st_scr[...] = jnp.zeros_like(st_scr)

    dskip = jnp.where(d == 0, 1.0, 0.0) * dsk_ref[...]

    def body(k, carry):
        sub = jnp.where(d == 0, k, n_sub - 1 - k)
        _ssd_chunk(d, pl.multiple_of(sub * CHUNK, CHUNK), xc_ref, dt_ref, cs_ref, dskip, y_ref, st_scr)
        return carry

    lax.fori_loop(0, n_sub, body, 0)


def _ssd(xc, dt, cs, d_skip, rows_per_step):
    b, l, _ = xc.shape
    tr = rows_per_step
    nb = l // tr

    def blk(d, c):
        return jnp.where(d == 0, c, nb - 1 - c)

    return pl.pallas_call(
        _ssd_kernel,
        grid=(b, 2, nb),
        in_specs=[
            pl.BlockSpec((1, tr, CONV_CH), lambda i, d, c: (i, blk(d, c), 0)),
            pl.BlockSpec((1, tr, LANES), lambda i, d, c: (i, blk(d, c), d)),
            pl.BlockSpec((1, tr, LANES), lambda i, d, c: (i, blk(d, c), d)),
            _resident((1, SSD_INNER)),
        ],
        out_specs=pl.BlockSpec((1, 1, tr, SSD_INNER), lambda i, d, c: (d, i, blk(d, c), 0)),
        out_shape=jax.ShapeDtypeStruct((2, b, l, SSD_INNER), BF16),
        scratch_shapes=[pltpu.VMEM((SSD_STATE, SSD_INNER), F32)],
        compiler_params=_cparams("parallel", "arbitrary", "arbitrary"),
        name="ssd_scan",
    )(xc, dt, cs, d_skip)


NA_UNROLL = 32
NA_HEADS_PER_STEP = 2


def _na_kernel(q_ref, k_ref, v_ref, strip_ref, o_ref, *, rows):
    win = NA_KR * GRID_W
    c1 = (NA_HEAD_DIM ** -0.5) * LOG2E

    def body(hh, it, carry):
        hs = slice(hh * NA_HEAD_DIM, (hh + 1) * NA_HEAD_DIM)
        q0s, k0s, scores = [], [], []
        for u in range(NA_UNROLL):
            r = it * NA_UNROLL + u
            rs = jnp.clip(r - NA_KR // 2, 0, rows - NA_KR)
            q0 = pl.multiple_of(r * GRID_W, GRID_W)
            k0 = pl.multiple_of(rs * GRID_W, GRID_W)
            q0s.append(q0)
            k0s.append(k0)
            scores.append(_dot_nt(q_ref[0, pl.ds(q0, GRID_W), hs], k_ref[0, pl.ds(k0, win), hs]) * c1
                          + strip_ref[hh, rs - r + NA_KR - 1])
        probs, dens = [], []
        for s in scores:
            m = jnp.max(s, axis=-1, keepdims=True)
            p = jnp.exp2(s - m)
            dens.append(jnp.sum(p, axis=-1, keepdims=True))
            probs.append(p.astype(BF16))
        for u in range(NA_UNROLL):
            o = _dot(probs[u], v_ref[0, pl.ds(k0s[u], win), hs]) / dens[u]
            o_ref[0, pl.ds(q0s[u], GRID_W), hs] = o.astype(o_ref.dtype)
        return carry

    for hh in range(NA_HEADS_PER_STEP):
        lax.fori_loop(0, rows // NA_UNROLL, functools.partial(body, hh), 0)


def _na(proj, strips, q_block, k_block, v_block):
    b, l, _ = proj.shape
    rows = l // GRID_W
    win = NA_KR * GRID_W
    hps = NA_HEADS_PER_STEP
    wide = hps * NA_HEAD_DIM
    assert q_block % hps == 0 and k_block % hps == 0 and v_block % hps == 0
    assert rows % NA_UNROLL == 0
    kern = functools.partial(_na_kernel, rows=rows)
    return pl.pallas_call(
        kern,
        grid=(b, NA_HEADS // hps),
        in_specs=[
            pl.BlockSpec((1, l, wide), lambda bi, h: (bi, 0, q_block // hps + h)),
            pl.BlockSpec((1, l, wide), lambda bi, h: (bi, 0, k_block // hps + h)),
            pl.BlockSpec((1, l, wide), lambda bi, h: (bi, 0, v_block // hps + h)),
            pl.BlockSpec((hps, NA_KR, GRID_W, win), lambda bi, h: (h, 0, 0, 0)),
        ],
        out_specs=pl.BlockSpec((1, l, wide), lambda bi, h: (bi, 0, h)),
        out_shape=jax.ShapeDtypeStruct((b, l, NA_HEADS * NA_HEAD_DIM), BF16),
        compiler_params=_cparams("parallel", "parallel"),
        name="neighbourhood_attn",
    )(proj, proj, proj, strips)


def _na_strips(rpb):
    qc = np.arange(GRID_W)[:, None]
    kc = np.arange(GRID_W)[None, :]
    cstart = np.clip(qc - NA_KC // 2, 0, GRID_W - NA_KC)
    valid = (kc >= cstart) & (kc < cstart + NA_KC)
    dcidx = np.clip(kc - qc + NA_KC - 1, 0, 2 * NA_KC - 2)
    tb = jnp.where(valid[None, None], rpb[:, :, dcidx].astype(F32) * LOG2E, NEG)
    return jnp.stack(
        [jnp.concatenate([tb[:, o + j] for j in range(NA_KR)], axis=-1) for o in range(NA_KR)], axis=1)


def _merge_kernel(yf_ref, yb_ref, z_ref, yna_ref, gs_ref, gn_ref, gso_ref, bg_ref, w1_ref, w2_ref, o_ref):
    br_na = _dot(yna_ref[...], w2_ref[...])
    y = yf_ref[0].astype(F32) + yb_ref[0].astype(F32)
    zf = z_ref[...].astype(F32)
    u = _rms(y * (zf * jax.nn.sigmoid(zf)), gso_ref[...]).astype(BF16)
    br_ssd = _dot(u, w1_ref[...])
    g_na = jax.nn.sigmoid(gn_ref[...].astype(F32) + bg_ref[1:2, :])
    g_ssd = jax.nn.sigmoid(gs_ref[...].astype(F32) + bg_ref[0:1, :])
    o_ref[...] = (g_ssd * br_ssd + g_na * br_na).astype(o_ref.dtype)


def _merge(y2, proj, yna, g_ssd_out, b_gate, w1, w2, z_block, gs_block, gn_block, tm):
    _, m, d = y2.shape
    return pl.pallas_call(
        _merge_kernel,
        grid=(m // tm,),
        in_specs=[
            pl.BlockSpec((1, tm, d), lambda i: (0, i, 0)),
            pl.BlockSpec((1, tm, d), lambda i: (1, i, 0)),
            pl.BlockSpec((tm, d), lambda i: (i, z_block)),
            pl.BlockSpec((tm, d), lambda i: (i, 0)),
            pl.BlockSpec((tm, d), lambda i: (i, gs_block)),
            pl.BlockSpec((tm, d), lambda i: (i, gn_block)),
            _resident((1, d)),
            _resident((2, d)),
            _resident((d, d)),
            _resident((d, d)),
        ],
        out_specs=pl.BlockSpec((tm, d), lambda i: (i, 0)),
        out_shape=jax.ShapeDtypeStruct((m, d), BF16),
        compiler_params=_cparams("parallel"),
        name="branch_merge",
    )(y2, y2, proj, yna, proj, proj, g_ssd_out, b_gate, w1, w2)


def _out_xattn_kernel(a_ref, x_ref, kv_ref, wout_ref, gmix_ref, gpre_ref, wq_ref, wo_ref, gpost_ref, o_ref,
                      *, parts):
    tm = x_ref.shape[1]
    rows = [slice(i * tm // parts, (i + 1) * tm // parts) for i in range(parts)]
    width = CA_HEADS * CA_HEAD_DIM
    heads = [slice(h * CA_HEAD_DIM, (h + 1) * CA_HEAD_DIM) for h in range(CA_HEADS)]
    proj = [_dot(a_ref[0, r, :], wout_ref[...]) for r in rows]
    x1 = [x_ref[0, r, :] + _rms(pj, gmix_ref[...]) for r, pj in zip(rows, proj)]
    q = [_dot(_rms(xi, gpre_ref[...]).astype(BF16), wq_ref[...]).astype(BF16) for xi in x1]
    scores = [[_dot_nt(qi[:, hs], kv_ref[0, :, hs]) * (CA_HEAD_DIM ** -0.5) for hs in heads] for qi in q]
    probs, dens = [], []
    for sc in scores:
        pp, dd = [], []
        for s in sc:
            m = jnp.max(s, axis=-1, keepdims=True)
            p = jnp.exp(s - m)
            dd.append(jnp.sum(p, axis=-1, keepdims=True))
            pp.append(p.astype(BF16))
        probs.append(pp)
        dens.append(dd)
    att = [jnp.concatenate(
        [(_dot(pp[h], kv_ref[0, :, width + h * CA_HEAD_DIM:width + (h + 1) * CA_HEAD_DIM]) / dd[h]).astype(BF16)
         for h in range(CA_HEADS)], axis=1) for pp, dd in zip(probs, dens)]
    cproj = [_dot(o, wo_ref[...]) for o in att]
    for r, xi, cp in zip(rows, x1, cproj):
        o_ref[0, r, :] = xi + _rms(cp, gpost_ref[...])


def _out_xattn(a, x, kv, w_out, g_mix_post, g_pre, wq, wo, g_post, tm, parts):
    b, l, d = x.shape
    n_mem, kvw = kv.shape[1], kv.shape[2]
    width = wq.shape[1]
    return pl.pallas_call(
        functools.partial(_out_xattn_kernel, parts=parts),
        grid=(b, l // tm),
        in_specs=[
            pl.BlockSpec((1, tm, d), lambda bi, i: (bi, i, 0)),
            pl.BlockSpec((1, tm, d), lambda bi, i: (bi, i, 0)),
            pl.BlockSpec((1, n_mem, kvw), lambda bi, i: (bi, 0, 0)),
            _resident((d, d)),
            _resident((1, d)),
            _resident((1, d)),
            _resident((d, width)),
            _resident((width, d)),
            _resident((1, d)),
        ],
        out_specs=pl.BlockSpec((1, tm, d), lambda bi, i: (bi, i, 0)),
        out_shape=jax.ShapeDtypeStruct((b, l, d), F32),
        compiler_params=_cparams("parallel", "parallel"),
        name="out_proj_xattn",
    )(a, x, kv, w_out, g_mix_post, g_pre, wq, wo, g_post)


def _ffn_kernel(x_ref, gpre_ref, wg_ref, wu_ref, wd_ref, gpost_ref, o_ref, a_scr, acc_scr):
    f = pl.program_id(1)

    @pl.when(f == 0)
    def _():
        a_scr[...] = _rms(x_ref[...], gpre_ref[...]).astype(BF16)
        acc_scr[...] = jnp.zeros_like(acc_scr)

    a = a_scr[...]
    hg = _dot(a, wg_ref[...])
    hu = _dot(a, wu_ref[...])
    act = (hg * jax.nn.sigmoid(hg) * hu).astype(BF16)
    acc_scr[...] += _dot(act, wd_ref[...])

    @pl.when(f == pl.num_programs(1) - 1)
    def _():
        o_ref[...] = x_ref[...] + _rms(acc_scr[...], gpost_ref[...])


def _ffn(x, g_pre, w_gu, w_down, g_post, tm, tf):
    m, d = x.shape
    dff = w_down.shape[0]
    nf = dff // tf
    return pl.pallas_call(
        _ffn_kernel,
        grid=(m // tm, nf),
        in_specs=[
            pl.BlockSpec((tm, d), lambda i, f: (i, 0)),
            pl.BlockSpec((1, d), lambda i, f: (0, 0)),
            pl.BlockSpec((d, tf), lambda i, f: (0, f)),
            pl.BlockSpec((d, tf), lambda i, f: (0, nf + f)),
            pl.BlockSpec((tf, d), lambda i, f: (f, 0)),
            pl.BlockSpec((1, d), lambda i, f: (0, 0)),
        ],
        out_specs=pl.BlockSpec((tm, d), lambda i, f: (i, 0)),
        out_shape=jax.ShapeDtypeStruct((m, d), F32),
        scratch_shapes=[pltpu.VMEM((tm, d), BF16), pltpu.VMEM((tm, d), F32)],
        compiler_params=_cparams("parallel", "arbitrary"),
        name="swiglu_ffn",
    )(x, g_pre, w_gu, w_gu, w_down, g_post)


def _pick(n, target):
    t = min(n, target)
    while n % t:
        t //= 2
    return t


def _prepare(w_in, b_gate, conv_w, conv_b, a_log, dt_bias, d_skip, g_ssd_out, rpb,
             w_br_ssd, w_br_na, w_out, g_mix_pre, g_mix_post, g_ca_pre, g_mem, w_ca_q, w_ca_kv,
             w_ca_o, g_ca_post, g_ffn_pre, w_ffn_gu, w_ffn_down, g_ffn_post):
    d = w_in.shape[0]
    row = lambda v: v.reshape(1, -1).astype(F32)
    o1 = SSD_INNER
    o2 = o1 + CONV_CH
    o3 = o2 + 2 * SSD_HEADS
    lane_pad = jnp.zeros((d, LANES - SSD_HEADS), w_in.dtype)
    head_pad = jnp.zeros((2, LANES - SSD_HEADS), F32)
    lanes2 = lambda v: jnp.concatenate([v.astype(F32).reshape(2, SSD_HEADS), head_pad], axis=1).reshape(1, -1)
    return dict(
        w_main=jnp.concatenate([w_in[:, :o1].astype(BF16), w_in[:, o3:].astype(BF16),
                                w_in[:, o1:o2].astype(BF16)], axis=1),
        w_dt=jnp.concatenate([w_in[:, o2:o2 + SSD_HEADS], lane_pad, w_in[:, o2 + SSD_HEADS:o3], lane_pad],
                             axis=1).astype(BF16),
        a_log=lanes2(a_log), dt_bias=lanes2(dt_bias),
        conv_w=jnp.pad(conv_w.astype(F32), ((0, 8 - SSD_CONV), (0, 0))), conv_b=row(conv_b),
        d_skip=row(jnp.repeat(d_skip, SSD_HEAD_DIM)), g_ssd_out=row(g_ssd_out),
        strips=_na_strips(rpb), b_gate=b_gate.astype(F32),
        w_br_ssd=w_br_ssd.astype(BF16), w_br_na=w_br_na.astype(BF16), w_out=w_out.astype(BF16),
        g_mix_pre=row(g_mix_pre), g_mix_post=row(g_mix_post), g_ca_pre=row(g_ca_pre), g_mem=row(g_mem),
        w_ca_q=w_ca_q.astype(BF16), w_ca_kv=w_ca_kv.astype(BF16), w_ca_o=w_ca_o.astype(BF16),
        g_ca_post=row(g_ca_post), g_ffn_pre=row(g_ffn_pre), w_ffn_gu=w_ffn_gu.astype(BF16),
        w_ffn_down=w_ffn_down.astype(BF16), g_ffn_post=row(g_ffn_post))


def _layer(x, mem, p):
    b, l, d = x.shape
    m = b * l
    na_w = NA_HEADS * NA_HEAD_DIM
    z_block, q_col, k_col, v_col = 0, d, d + na_w, d + 2 * na_w
    gs_col = d + 3 * na_w
    gn_col = gs_col + d
    xbc_col = gn_col + d
    assert xbc_col % CONV_CH == 0 and gs_col % d == 0 and gn_col % d == 0

    x2d = x.reshape(m, d)
    proj, dt, cs = _in_proj(x2d, p["g_mix_pre"], p["w_main"], p["w_dt"], p["a_log"], p["dt_bias"],
                            _pick(l, 1024), 1536)

    proj3 = proj.reshape(b, l, -1)
    xc = _conv_silu(proj3, p["conv_w"], p["conv_b"], xbc_col // CONV_CH, _pick(l, 512))
    y2 = _ssd(xc, dt.reshape(b, l, -1), cs.reshape(b, l, -1), p["d_skip"], _pick(l, 1024))

    y_na = _na(proj3, p["strips"], q_col // NA_HEAD_DIM, k_col // NA_HEAD_DIM, v_col // NA_HEAD_DIM)

    merged = _merge(y2.reshape(2, m, SSD_INNER), proj, y_na.reshape(m, na_w), p["g_ssd_out"],
                    p["b_gate"], p["w_br_ssd"], p["w_br_na"],
                    z_block, gs_col // d, gn_col // d, _pick(m, 256))
    n_mem = mem.shape[1]
    kv = _norm_matmul(mem.reshape(b * n_mem, d), p["g_mem"], p["w_ca_kv"], BF16, _pick(b * n_mem, 256), 1024)
    x2 = _out_xattn(merged.reshape(b, l, d), x, kv.reshape(b, n_mem, -1), p["w_out"], p["g_mix_post"],
                    p["g_ca_pre"], p["w_ca_q"], p["w_ca_o"], p["g_ca_post"], _pick(l, 512), 2)

    x3 = _ffn(x2.reshape(m, d), p["g_ffn_pre"], p["w_ffn_gu"], p["w_ffn_down"], p["g_ffn_post"],
              _pick(m, 512), 512)
    return x3.reshape(b, l, d)


def kernel(x_prompt, x_sample, mem_prompt, mem_sample, w_in, b_gate, conv_w, conv_b, a_log, dt_bias, d_skip, g_ssd_out, rpb, w_br_ssd, w_br_na, w_out, g_mix_pre, g_mix_post, g_ca_pre, g_mem, w_ca_q, w_ca_kv, w_ca_o, g_ca_post, g_ffn_pre, w_ffn_gu, w_ffn_down, g_ffn_post):
    params = (w_in, b_gate, conv_w, conv_b, a_log, dt_bias, d_skip, g_ssd_out, rpb, w_br_ssd,
              w_br_na, w_out, g_mix_pre, g_mix_post, g_ca_pre, g_mem, w_ca_q, w_ca_kv, w_ca_o,
              g_ca_post, g_ffn_pre, w_ffn_gu, w_ffn_down, g_ffn_post)
    y_prompt, y_sample = x_prompt, x_sample
    for i in range(w_in.shape[0]):
        p = _prepare(*[w[i] for w in params])
        y_prompt = _layer(y_prompt, mem_prompt, p)
        y_sample = _layer(y_sample, mem_sample, p)
    return (y_prompt, y_sample)
```

```python
import functools
import math

import numpy as np
import jax
import jax.numpy as jnp
from jax import lax
from jax.experimental import pallas as pl
from jax.experimental.pallas import tpu as pltpu

BF16 = jnp.bfloat16
F32 = jnp.float32

EPS = 1e-6
LANES = 128
GRID_W = 64
SSD_HEADS = 32
SSD_HEAD_DIM = 64
SSD_GROUPS = 4
SSD_STATE = 128
SSD_INNER = SSD_HEADS * SSD_HEAD_DIM
SSD_CONV = 5
CONV_CH = SSD_INNER + 2 * SSD_GROUPS * SSD_STATE
CHUNK = 128
HALO = 16
NA_HEADS = 16
NA_HEAD_DIM = 128
NA_KR = 8
NA_KC = 16
CA_HEADS = 4
CA_HEAD_DIM = 128
NEG = -1e30
LOG2E = math.log2(math.e)

VMEM_LIMIT_BYTES = 56 * 1024 * 1024


def _cparams(*sem):
    return pltpu.CompilerParams(dimension_semantics=sem, vmem_limit_bytes=VMEM_LIMIT_BYTES)


def _resident(shape):
    return pl.BlockSpec(shape, lambda *_: (0,) * len(shape), pipeline_mode=pl.Buffered(1))


def _rms(x, g):
    return x * lax.rsqrt(jnp.mean(x * x, axis=-1, keepdims=True) + EPS) * g


def _dot(a, b):
    return jnp.dot(a, b, preferred_element_type=F32)


def _dot_nt(a, b):
    return lax.dot_general(a, b, (((1,), (1,)), ((), ())), preferred_element_type=F32)


def _norm_matmul_kernel(x_ref, g_ref, w_ref, o_ref, a_scr):
    @pl.when(pl.program_id(1) == 0)
    def _():
        a_scr[...] = _rms(x_ref[...], g_ref[...]).astype(BF16)

    o_ref[...] = _dot(a_scr[...], w_ref[...]).astype(o_ref.dtype)


def _norm_matmul(x, g, w, out_dtype, tm, tn):
    m, d = x.shape
    n = w.shape[1]
    return pl.pallas_call(
        _norm_matmul_kernel,
        grid=(m // tm, n // tn),
        in_specs=[
            pl.BlockSpec((tm, d), lambda i, j: (i, 0)),
            pl.BlockSpec((1, d), lambda i, j: (0, 0)),
            pl.BlockSpec((d, tn), lambda i, j: (0, j)),
        ],
        out_specs=pl.BlockSpec((tm, tn), lambda i, j: (i, j)),
        out_shape=jax.ShapeDtypeStruct((m, n), out_dtype),
        scratch_shapes=[pltpu.VMEM((tm, d), BF16)],
        compiler_params=_cparams("parallel", "arbitrary"),
        name="norm_matmul",
    )(x, g, w)


def _split3(x):
    hi = x.astype(BF16)
    r1 = x - hi.astype(F32)
    mid = r1.astype(BF16)
    lo = (r1 - mid.astype(F32)).astype(BF16)
    return hi, mid, lo


SCAN_ROWS = 512
SCAN_EXP, SCAN_LOG, SCAN_SCALE = jnp.exp2, jnp.log2, LOG2E


def _dt_scan(raw, alog_ref, dtb_ref, dt_ref, cs_ref):
    ri = lax.broadcasted_iota(jnp.int32, (SCAN_ROWS, SCAN_ROWS), 0)
    ci = lax.broadcasted_iota(jnp.int32, (SCAN_ROWS, SCAN_ROWS), 1)
    same = (ri // CHUNK) == (ci // CHUNK)
    t_fwd = (same & (ri >= ci)).astype(BF16)
    t_bwd = (same & (ri <= ci)).astype(BF16)
    a_neg = -jnp.exp(alog_ref[...])
    for r0 in range(0, raw.shape[0], SCAN_ROWS):
        rs = slice(r0, r0 + SCAN_ROWS)
        z = raw[rs] + dtb_ref[...]
        dt = jnp.maximum(z, 0.0) + jnp.log1p(jnp.exp(-jnp.abs(z)))
        hi, mid, lo = _split3(dt * a_neg)
        dt_ref[rs, :] = dt
        cs_ref[rs, :LANES] = SCAN_SCALE * (
            _dot(t_fwd, hi[:, :LANES]) + _dot(t_fwd, mid[:, :LANES]) + _dot(t_fwd, lo[:, :LANES]))
        cs_ref[rs, LANES:] = SCAN_SCALE * (
            _dot(t_bwd, hi[:, LANES:]) + _dot(t_bwd, mid[:, LANES:]) + _dot(t_bwd, lo[:, LANES:]))


def _in_proj_kernel(x_ref, g_ref, w_ref, wdt_ref, alog_ref, dtb_ref, o_ref, dt_ref, cs_ref, a_scr):
    @pl.when(pl.program_id(1) == 0)
    def _():
        a = _rms(x_ref[...], g_ref[...]).astype(BF16)
        a_scr[...] = a
        raw = _dot(a, wdt_ref[...])
        o_ref[...] = _dot(a, w_ref[...]).astype(o_ref.dtype)
        _dt_scan(raw, alog_ref, dtb_ref, dt_ref, cs_ref)

    @pl.when(pl.program_id(1) != 0)
    def _():
        o_ref[...] = _dot(a_scr[...], w_ref[...]).astype(o_ref.dtype)


def _in_proj(x, g, w, w_dt, a_log, dt_bias, tm, tn):
    m, d = x.shape
    n = w.shape[1]
    nd = 2 * LANES
    small = jax.ShapeDtypeStruct((m, nd), F32)
    return pl.pallas_call(
        _in_proj_kernel,
        grid=(m // tm, n // tn),
        in_specs=[
            pl.BlockSpec((tm, d), lambda i, j: (i, 0)),
            _resident((1, d)),
            pl.BlockSpec((d, tn), lambda i, j: (0, j)),
            _resident((d, nd)),
            _resident((1, nd)),
            _resident((1, nd)),
        ],
        out_specs=[pl.BlockSpec((tm, tn), lambda i, j: (i, j)),
                   pl.BlockSpec((tm, nd), lambda i, j: (i, 0)),
                   pl.BlockSpec((tm, nd), lambda i, j: (i, 0))],
        out_shape=[jax.ShapeDtypeStruct((m, n), BF16), small, small],
        scratch_shapes=[pltpu.VMEM((tm, d), BF16)],
        compiler_params=_cparams("parallel", "arbitrary"),
        name="in_proj",
    )(x, g, w, w_dt, a_log, dt_bias)


def _conv_kernel(xm_ref, xp_ref, xn_ref, cw_ref, cb_ref, o_ref, xs_scr, *, n_tiles):
    i = pl.program_id(1)
    tc = xm_ref.shape[1]
    xs_scr[0:HALO, :] = xp_ref[0] * (i > 0).astype(BF16)
    xs_scr[HALO:HALO + tc, :] = xm_ref[0]
    xs_scr[HALO + tc:, :] = xn_ref[0] * (i < n_tiles - 1).astype(BF16)
    win = CHUNK + 2 * HALO
    taps = [k for k in range(SSD_CONV) if k != SSD_CONV // 2]
    ri = lax.broadcasted_iota(jnp.int32, (len(taps) * CHUNK, win), 0)
    ci = lax.broadcasted_iota(jnp.int32, (len(taps) * CHUNK, win), 1)
    shift = jnp.zeros_like(ri)
    for n, k in enumerate(taps):
        shift = jnp.where(ri // CHUNK == n, k - SSD_CONV // 2, shift)
    select = (ci == ri % CHUNK + HALO + shift).astype(BF16)
    for r in range(tc // CHUNK):
        window = xs_scr[r * CHUNK:r * CHUNK + win, :]
        shifted = _dot(select, window)
        acc = cb_ref[...] + cw_ref[SSD_CONV // 2:SSD_CONV // 2 + 1, :] * window[HALO:HALO + CHUNK].astype(F32)
        for n, k in enumerate(taps):
            acc = acc + cw_ref[k:k + 1, :] * shifted[n * CHUNK:(n + 1) * CHUNK]
        o_ref[0, r * CHUNK:(r + 1) * CHUNK, :] = (acc * jax.nn.sigmoid(acc)).astype(o_ref.dtype)


def _conv_silu(proj, conv_w, conv_b, xbc_block, tc):
    b, l, _ = proj.shape
    n_tiles = l // tc
    hpt = tc // HALO
    kern = functools.partial(_conv_kernel, n_tiles=n_tiles)
    return pl.pallas_call(
        kern,
        grid=(b, n_tiles),
        in_specs=[
            pl.BlockSpec((1, tc, CONV_CH), lambda bi, i: (bi, i, xbc_block)),
            pl.BlockSpec((1, HALO, CONV_CH), lambda bi, i: (bi, jnp.maximum(i * hpt - 1, 0), xbc_block)),
            pl.BlockSpec((1, HALO, CONV_CH),
                         lambda bi, i: (bi, jnp.minimum((i + 1) * hpt, n_tiles * hpt - 1), xbc_block)),
            pl.BlockSpec((8, CONV_CH), lambda bi, i: (0, 0)),
            pl.BlockSpec((1, CONV_CH), lambda bi, i: (0, 0)),
        ],
        out_specs=pl.BlockSpec((1, tc, CONV_CH), lambda bi, i: (bi, i, 0)),
        out_shape=jax.ShapeDtypeStruct((b, l, CONV_CH), BF16),
        scratch_shapes=[pltpu.VMEM((tc + 2 * HALO, CONV_CH), BF16)],
        compiler_params=_cparams("parallel", "parallel"),
        name="conv_silu",
    )(proj, proj, proj, conv_w, conv_b)


class _Scan:
    def __init__(self, d, r0, xc_ref, dt_ref, cs_ref, y_ref, st_scr, dskip):
        self.rows = rows = pl.ds(r0, CHUNK)
        self.xc_ref, self.y_ref, self.st_scr, self.dskip = xc_ref, y_ref, st_scr, dskip
        self.cs = cs = cs_ref[0, rows, :]
        cs_t = cs.T
        dt_t = dt_ref[0, rows, :].T
        tot_col = cs_t[:, CHUNK - 1:CHUNK] if d == 0 else cs_t[:, 0:1]
        tot_b = jnp.broadcast_to(tot_col, (CHUNK, CHUNK))
        self.wst_t = SCAN_EXP(tot_b - cs_t) * dt_t
        self.cdec = SCAN_EXP(tot_b)
        self.csd_t = cs_t - SCAN_LOG(dt_t)
        ri = lax.broadcasted_iota(jnp.int32, (CHUNK, CHUNK), 0)
        ci = lax.broadcasted_iota(jnp.int32, (CHUNK, CHUNK), 1)
        self.causal = (ri >= ci) if d == 0 else (ri <= ci)
        self.first = lax.broadcasted_iota(jnp.int32, (1, 2 * SSD_HEAD_DIM), 1) < SSD_HEAD_DIM

    def group(self, g):
        gn = SSD_GROUPS * SSD_STATE
        b_g = self.xc_ref[0, self.rows, SSD_INNER + g * SSD_STATE:SSD_INNER + (g + 1) * SSD_STATE]
        c_g = self.xc_ref[0, self.rows, SSD_INNER + gn + g * SSD_STATE:SSD_INNER + gn + (g + 1) * SSD_STATE]
        self.gmat = _dot_nt(c_g, b_g)
        self.c_f = c_g.astype(F32)
        self.b_t = b_g.astype(F32).T

    def pair(self, h0):
        rows, first = self.rows, self.first
        ps = slice(h0 * SSD_HEAD_DIM, (h0 + 2) * SSD_HEAD_DIM)
        x2 = self.xc_ref[0, rows, ps]
        s2 = self.st_scr[:, ps]
        s2_bf = s2.astype(BF16)
        zero = jnp.zeros_like(x2)
        lhs, bw, rhs, xs = [], [], [], []
        for e in range(2):
            h = h0 + e
            col = jnp.broadcast_to(self.cs[:, h:h + 1], (CHUNK, CHUNK))
            lmat_dt = SCAN_EXP(jnp.where(self.causal, col - self.csd_t[h:h + 1, :], NEG))
            lhs.append((self.gmat * lmat_dt).astype(BF16))
            lhs.append((self.c_f * SCAN_EXP(col)).astype(BF16))
            bw.append((self.b_t * self.wst_t[h:h + 1, :]).astype(BF16))
            keep = first if e == 0 else jnp.logical_not(first)
            xe = jnp.where(keep, x2, zero)
            xs.append(xe)
            rhs.append(xe)
            rhs.append(jnp.where(keep, s2_bf, zero))
        y = _dot(jnp.concatenate(lhs, axis=1), jnp.concatenate(rhs, axis=0))
        if self.dskip is not None:
            y = y + self.dskip[:, ps] * x2.astype(F32)
        self.y_ref[0, rows, ps] = y.astype(self.y_ref.dtype)
        s_new = _dot(jnp.concatenate(bw, axis=1), jnp.concatenate(xs, axis=0))
        cd = jnp.where(first, self.cdec[h0:h0 + 1, :], self.cdec[h0 + 1:h0 + 2, :])
        self.st_scr[:, ps] = s2 * cd + s_new


def _ssd_kernel(xcf_ref, dtf_ref, csf_ref, xcb_ref, dtb_ref, csb_ref, dsk_ref, yf_ref, yb_ref, stf_scr, stb_scr):
    n_sub = xcf_ref.shape[1] // CHUNK

    @pl.when(pl.program_id(1) == 0)
    def _():
        stf_scr[...] = jnp.zeros_like(stf_scr)
        stb_scr[...] = jnp.zeros_like(stb_scr)

    dskip = dsk_ref[...]
    pairs = SSD_HEADS // SSD_GROUPS // 2

    def body(k, carry):
        fwd = _Scan(0, pl.multiple_of(k * CHUNK, CHUNK), xcf_ref, dtf_ref, csf_ref, yf_ref, stf_scr, dskip)
        bwd = _Scan(1, pl.multiple_of((n_sub - 1 - k) * CHUNK, CHUNK), xcb_ref, dtb_ref, csb_ref, yb_ref,
                    stb_scr, None)
        for g in range(SSD_GROUPS):
            fwd.group(g)
            bwd.group(g)
            for j in range(pairs):
                fwd.pair((g * pairs + j) * 2)
                bwd.pair((g * pairs + j) * 2)
        return carry

    lax.fori_loop(0, n_sub, body, 0)


def _ssd(xc, dt, cs, d_skip, rows_per_step):
    b, l, _ = xc.shape
    tr = rows_per_step
    nb = l // tr
    out = jax.ShapeDtypeStruct((b, l, SSD_INNER), BF16)
    return pl.pallas_call(
        _ssd_kernel,
        grid=(b, nb),
        in_specs=[
            pl.BlockSpec((1, tr, CONV_CH), lambda i, c: (i, c, 0)),
            pl.BlockSpec((1, tr, LANES), lambda i, c: (i, c, 0)),
            pl.BlockSpec((1, tr, LANES), lambda i, c: (i, c, 0)),
            pl.BlockSpec((1, tr, CONV_CH), lambda i, c: (i, nb - 1 - c, 0)),
            pl.BlockSpec((1, tr, LANES), lambda i, c: (i, nb - 1 - c, 1)),
            pl.BlockSpec((1, tr, LANES), lambda i, c: (i, nb - 1 - c, 1)),
            _resident((1, SSD_INNER)),
        ],
        out_specs=[pl.BlockSpec((1, tr, SSD_INNER), lambda i, c: (i, c, 0)),
                   pl.BlockSpec((1, tr, SSD_INNER), lambda i, c: (i, nb - 1 - c, 0))],
        out_shape=[out, out],
        scratch_shapes=[pltpu.VMEM((SSD_STATE, SSD_INNER), F32), pltpu.VMEM((SSD_STATE, SSD_INNER), F32)],
        compiler_params=_cparams("parallel", "arbitrary"),
        name="ssd_scan",
    )(xc, dt, cs, xc, dt, cs, d_skip)


NA_UNROLL = 32
NA_HEADS_PER_STEP = 2


def _na_kernel(q_ref, k_ref, v_ref, strip_ref, o_ref, *, rows):
    win = NA_KR * GRID_W
    c1 = (NA_HEAD_DIM ** -0.5) * LOG2E

    def body(hh, it, carry):
        hs = slice(hh * NA_HEAD_DIM, (hh + 1) * NA_HEAD_DIM)
        q0s, k0s, scores = [], [], []
        for u in range(NA_UNROLL):
            r = it * NA_UNROLL + u
            rs = jnp.clip(r - NA_KR // 2, 0, rows - NA_KR)
            q0 = pl.multiple_of(r * GRID_W, GRID_W)
            k0 = pl.multiple_of(rs * GRID_W, GRID_W)
            q0s.append(q0)
            k0s.append(k0)
            scores.append(_dot_nt(q_ref[0, pl.ds(q0, GRID_W), hs], k_ref[0, pl.ds(k0, win), hs]) * c1
                          + strip_ref[hh, rs - r + NA_KR - 1])
        probs, dens = [], []
        for s in scores:
            m = jnp.max(s, axis=-1, keepdims=True)
            p = jnp.exp2(s - m)
            dens.append(jnp.sum(p, axis=-1, keepdims=True))
            probs.append(p.astype(BF16))
        for u in range(NA_UNROLL):
            o = _dot(probs[u], v_ref[0, pl.ds(k0s[u], win), hs]) / dens[u]
            o_ref[0, pl.ds(q0s[u], GRID_W), hs] = o.astype(o_ref.dtype)
        return carry

    for hh in range(NA_HEADS_PER_STEP):
        lax.fori_loop(0, rows // NA_UNROLL, functools.partial(body, hh), 0)


def _na(proj, strips, q_block, k_block, v_block):
    b, l, _ = proj.shape
    rows = l // GRID_W
    win = NA_KR * GRID_W
    hps = NA_HEADS_PER_STEP
    wide = hps * NA_HEAD_DIM
    assert q_block % hps == 0 and k_block % hps == 0 and v_block % hps == 0
    assert rows % NA_UNROLL == 0
    kern = functools.partial(_na_kernel, rows=rows)
    return pl.pallas_call(
        kern,
        grid=(b, NA_HEADS // hps),
        in_specs=[
            pl.BlockSpec((1, l, wide), lambda bi, h: (bi, 0, q_block // hps + h)),
            pl.BlockSpec((1, l, wide), lambda bi, h: (bi, 0, k_block // hps + h)),
            pl.BlockSpec((1, l, wide), lambda bi, h: (bi, 0, v_block // hps + h)),
            pl.BlockSpec((hps, NA_KR, GRID_W, win), lambda bi, h: (h, 0, 0, 0)),
        ],
        out_specs=pl.BlockSpec((1, l, wide), lambda bi, h: (bi, 0, h)),
        out_shape=jax.ShapeDtypeStruct((b, l, NA_HEADS * NA_HEAD_DIM), BF16),
        compiler_params=_cparams("parallel", "parallel"),
        name="neighbourhood_attn",
    )(proj, proj, proj, strips)


def _na_strips(rpb):
    qc = np.arange(GRID_W)[:, None]
    kc = np.arange(GRID_W)[None, :]
    cstart = np.clip(qc - NA_KC // 2, 0, GRID_W - NA_KC)
    valid = (kc >= cstart) & (kc < cstart + NA_KC)
    dcidx = np.clip(kc - qc + NA_KC - 1, 0, 2 * NA_KC - 2)
    tb = jnp.where(valid[None, None], rpb[:, :, dcidx].astype(F32) * LOG2E, NEG)
    return jnp.stack(
        [jnp.concatenate([tb[:, o + j] for j in range(NA_KR)], axis=-1) for o in range(NA_KR)], axis=1)


def _merge_kernel(yf_ref, yb_ref, z_ref, yna_ref, gs_ref, gn_ref, gso_ref, bg_ref, w1_ref, w2_ref, o_ref):
    br_na = _dot(yna_ref[...], w2_ref[...])
    y = yf_ref[...].astype(F32) + yb_ref[...].astype(F32)
    zf = z_ref[...].astype(F32)
    u = _rms(y * (zf * jax.nn.sigmoid(zf)), gso_ref[...]).astype(BF16)
    br_ssd = _dot(u, w1_ref[...])
    g_na = jax.nn.sigmoid(gn_ref[...].astype(F32) + bg_ref[1:2, :])
    g_ssd = jax.nn.sigmoid(gs_ref[...].astype(F32) + bg_ref[0:1, :])
    o_ref[...] = (g_ssd * br_ssd + g_na * br_na).astype(o_ref.dtype)


def _merge(yf, yb, proj, yna, g_ssd_out, b_gate, w1, w2, z_block, gs_block, gn_block, tm):
    m, d = yf.shape
    return pl.pallas_call(
        _merge_kernel,
        grid=(m // tm,),
        in_specs=[
            pl.BlockSpec((tm, d), lambda i: (i, 0)),
            pl.BlockSpec((tm, d), lambda i: (i, 0)),
            pl.BlockSpec((tm, d), lambda i: (i, z_block)),
            pl.BlockSpec((tm, d), lambda i: (i, 0)),
            pl.BlockSpec((tm, d), lambda i: (i, gs_block)),
            pl.BlockSpec((tm, d), lambda i: (i, gn_block)),
            _resident((1, d)),
            _resident((2, d)),
            _resident((d, d)),
            _resident((d, d)),
        ],
        out_specs=pl.BlockSpec((tm, d), lambda i: (i, 0)),
        out_shape=jax.ShapeDtypeStruct((m, d), BF16),
        compiler_params=_cparams("parallel"),
        name="branch_merge",
    )(yf, yb, proj, yna, proj, proj, g_ssd_out, b_gate, w1, w2)


def _out_xattn_kernel(a_ref, x_ref, kv_ref, wout_ref, gmix_ref, gpre_ref, wq_ref, wo_ref, gpost_ref, o_ref,
                      *, parts):
    tm = x_ref.shape[1]
    rows = [slice(i * tm // parts, (i + 1) * tm // parts) for i in range(parts)]
    width = CA_HEADS * CA_HEAD_DIM
    heads = [slice(h * CA_HEAD_DIM, (h + 1) * CA_HEAD_DIM) for h in range(CA_HEADS)]
    proj = [_dot(a_ref[0, r, :], wout_ref[...]) for r in rows]
    x1 = [x_ref[0, r, :] + _rms(pj, gmix_ref[...]) for r, pj in zip(rows, proj)]
    q = [_dot(_rms(xi, gpre_ref[...]).astype(BF16), wq_ref[...]).astype(BF16) for xi in x1]
    scores = [[_dot_nt(qi[:, hs], kv_ref[0, :, hs]) * (CA_HEAD_DIM ** -0.5) for hs in heads] for qi in q]
    probs, dens = [], []
    for sc in scores:
        pp, dd = [], []
        for s in sc:
            m = jnp.max(s, axis=-1, keepdims=True)
            p = jnp.exp(s - m)
            dd.append(jnp.sum(p, axis=-1, keepdims=True))
            pp.append(p.astype(BF16))
        probs.append(pp)
        dens.append(dd)
    att = [jnp.concatenate(
        [(_dot(pp[h], kv_ref[0, :, width + h * CA_HEAD_DIM:width + (h + 1) * CA_HEAD_DIM]) / dd[h]).astype(BF16)
         for h in range(CA_HEADS)], axis=1) for pp, dd in zip(probs, dens)]
    cproj = [_dot(o, wo_ref[...]) for o in att]
    for r, xi, cp in zip(rows, x1, cproj):
        o_ref[0, r, :] = xi + _rms(cp, gpost_ref[...])


def _out_xattn(a, x, kv, w_out, g_mix_post, g_pre, wq, wo, g_post, tm, parts):
    b, l, d = x.shape
    n_mem, kvw = kv.shape[1], kv.shape[2]
    width = wq.shape[1]
    return pl.pallas_call(
        functools.partial(_out_xattn_kernel, parts=parts),
        grid=(b, l // tm),
        in_specs=[
            pl.BlockSpec((1, tm, d), lambda bi, i: (bi, i, 0)),
            pl.BlockSpec((1, tm, d), lambda bi, i: (bi, i, 0)),
            pl.BlockSpec((1, n_mem, kvw), lambda bi, i: (bi, 0, 0)),
            _resident((d, d)),
            _resident((1, d)),
            _resident((1, d)),
            _resident((d, width)),
            _resident((width, d)),
            _resident((1, d)),
        ],
        out_specs=pl.BlockSpec((1, tm, d), lambda bi, i: (bi, i, 0)),
        out_shape=jax.ShapeDtypeStruct((b, l, d), F32),
        compiler_params=_cparams("parallel", "parallel"),
        name="out_proj_xattn",
    )(a, x, kv, w_out, g_mix_post, g_pre, wq, wo, g_post)


def _ffn_kernel(x_ref, gpre_ref, wg_ref, wu_ref, wd_ref, gpost_ref, o_ref, a_scr, acc_scr):
    f = pl.program_id(1)

    @pl.when(f == 0)
    def _():
        a_scr[...] = _rms(x_ref[...], gpre_ref[...]).astype(BF16)
        acc_scr[...] = jnp.zeros_like(acc_scr)

    a = a_scr[...]
    hg = _dot(a, wg_ref[...])
    hu = _dot(a, wu_ref[...])
    act = (hg * jax.nn.sigmoid(hg) * hu).astype(BF16)
    acc_scr[...] += _dot(act, wd_ref[...])

    @pl.when(f == pl.num_programs(1) - 1)
    def _():
        o_ref[...] = x_ref[...] + _rms(acc_scr[...], gpost_ref[...])


def _ffn(x, g_pre, w_gu, w_down, g_post, tm, tf):
    m, d = x.shape
    dff = w_down.shape[0]
    nf = dff // tf
    return pl.pallas_call(
        _ffn_kernel,
        grid=(m // tm, nf),
        in_specs=[
            pl.BlockSpec((tm, d), lambda i, f: (i, 0)),
            pl.BlockSpec((1, d), lambda i, f: (0, 0)),
            pl.BlockSpec((d, tf), lambda i, f: (0, f)),
            pl.BlockSpec((d, tf), lambda i, f: (0, nf + f)),
            pl.BlockSpec((tf, d), lambda i, f: (f, 0)),
            pl.BlockSpec((1, d), lambda i, f: (0, 0)),
        ],
        out_specs=pl.BlockSpec((tm, d), lambda i, f: (i, 0)),
        out_shape=jax.ShapeDtypeStruct((m, d), F32),
        scratch_shapes=[pltpu.VMEM((tm, d), BF16), pltpu.VMEM((tm, d), F32)],
        compiler_params=_cparams("parallel", "arbitrary"),
        name="swiglu_ffn",
    )(x, g_pre, w_gu, w_gu, w_down, g_post)


def _pick(n, target):
    t = min(n, target)
    while n % t:
        t //= 2
    return t


def _prepare(w_in, b_gate, conv_w, conv_b, a_log, dt_bias, d_skip, g_ssd_out, rpb,
             w_br_ssd, w_br_na, w_out, g_mix_pre, g_mix_post, g_ca_pre, g_mem, w_ca_q, w_ca_kv,
             w_ca_o, g_ca_post, g_ffn_pre, w_ffn_gu, w_ffn_down, g_ffn_post):
    d = w_in.shape[0]
    row = lambda v: v.reshape(1, -1).astype(F32)
    o1 = SSD_INNER
    o2 = o1 + CONV_CH
    o3 = o2 + 2 * SSD_HEADS
    lane_pad = jnp.zeros((d, LANES - SSD_HEADS), w_in.dtype)
    head_pad = jnp.zeros((2, LANES - SSD_HEADS), F32)
    lanes2 = lambda v: jnp.concatenate([v.astype(F32).reshape(2, SSD_HEADS), head_pad], axis=1).reshape(1, -1)
    return dict(
        w_main=jnp.concatenate([w_in[:, :o1].astype(BF16), w_in[:, o3:].astype(BF16),
                                w_in[:, o1:o2].astype(BF16)], axis=1),
        w_dt=jnp.concatenate([w_in[:, o2:o2 + SSD_HEADS], lane_pad, w_in[:, o2 + SSD_HEADS:o3], lane_pad],
                             axis=1).astype(BF16),
        a_log=lanes2(a_log), dt_bias=lanes2(dt_bias),
        conv_w=jnp.pad(conv_w.astype(F32), ((0, 8 - SSD_CONV), (0, 0))), conv_b=row(conv_b),
        d_skip=row(jnp.repeat(d_skip, SSD_HEAD_DIM)), g_ssd_out=row(g_ssd_out),
        strips=_na_strips(rpb), b_gate=b_gate.astype(F32),
        w_br_ssd=w_br_ssd.astype(BF16), w_br_na=w_br_na.astype(BF16), w_out=w_out.astype(BF16),
        g_mix_pre=row(g_mix_pre), g_mix_post=row(g_mix_post), g_ca_pre=row(g_ca_pre), g_mem=row(g_mem),
        w_ca_q=w_ca_q.astype(BF16), w_ca_kv=w_ca_kv.astype(BF16), w_ca_o=w_ca_o.astype(BF16),
        g_ca_post=row(g_ca_post), g_ffn_pre=row(g_ffn_pre), w_ffn_gu=w_ffn_gu.astype(BF16),
        w_ffn_down=w_ffn_down.astype(BF16), g_ffn_post=row(g_ffn_post))


def _layer(x, mem, p):
    b, l, d = x.shape
    m = b * l
    na_w = NA_HEADS * NA_HEAD_DIM
    z_block, q_col, k_col, v_col = 0, d, d + na_w, d + 2 * na_w
    gs_col = d + 3 * na_w
    gn_col = gs_col + d
    xbc_col = gn_col + d
    assert xbc_col % CONV_CH == 0 and gs_col % d == 0 and gn_col % d == 0

    x2d = x.reshape(m, d)
    proj, dt, cs = _in_proj(x2d, p["g_mix_pre"], p["w_main"], p["w_dt"], p["a_log"], p["dt_bias"],
                            _pick(l, 1024), 1536)

    proj3 = proj.reshape(b, l, -1)
    xc = _conv_silu(proj3, p["conv_w"], p["conv_b"], xbc_col // CONV_CH, _pick(l, 512))
    yf, yb = _ssd(xc, dt.reshape(b, l, -1), cs.reshape(b, l, -1), p["d_skip"], _pick(l, 512))

    y_na = _na(proj3, p["strips"], q_col // NA_HEAD_DIM, k_col // NA_HEAD_DIM, v_col // NA_HEAD_DIM)

    merged = _merge(yf.reshape(m, SSD_INNER), yb.reshape(m, SSD_INNER), proj, y_na.reshape(m, na_w), p["g_ssd_out"],
                    p["b_gate"], p["w_br_ssd"], p["w_br_na"],
                    z_block, gs_col // d, gn_col // d, _pick(m, 256))
    n_mem = mem.shape[1]
    kv = _norm_matmul(mem.reshape(b * n_mem, d), p["g_mem"], p["w_ca_kv"], BF16, _pick(b * n_mem, 256), 1024)
    x2 = _out_xattn(merged.reshape(b, l, d), x, kv.reshape(b, n_mem, -1), p["w_out"], p["g_mix_post"],
                    p["g_ca_pre"], p["w_ca_q"], p["w_ca_o"], p["g_ca_post"], _pick(l, 512), 2)

    x3 = _ffn(x2.reshape(m, d), p["g_ffn_pre"], p["w_ffn_gu"], p["w_ffn_down"], p["g_ffn_post"],
              _pick(m, 512), 512)
    return x3.reshape(b, l, d)


def kernel(x_prompt, x_sample, mem_prompt, mem_sample, w_in, b_gate, conv_w, conv_b, a_log, dt_bias, d_skip, g_ssd_out, rpb, w_br_ssd, w_br_na, w_out, g_mix_pre, g_mix_post, g_ca_pre, g_mem, w_ca_q, w_ca_kv, w_ca_o, g_ca_post, g_ffn_pre, w_ffn_gu, w_ffn_down, g_ffn_post):
    params = (w_in, b_gate, conv_w, conv_b, a_log, dt_bias, d_skip, g_ssd_out, rpb, w_br_ssd,
              w_br_na, w_out, g_mix_pre, g_mix_post, g_ca_pre, g_mem, w_ca_q, w_ca_kv, w_ca_o,
              g_ca_post, g_ffn_pre, w_ffn_gu, w_ffn_down, g_ffn_post)
    y_prompt, y_sample = x_prompt, x_sample
    for i in range(w_in.shape[0]):
        p = _prepare(*[w[i] for w in params])
        y_prompt = _layer(y_prompt, mem_prompt, p)
        y_sample = _layer(y_sample, mem_sample, p)
    return (y_prompt, y_sample)
```
